```python
import math
import jax, jax.numpy as jnp
from jax import lax
import numpy as np

D_MODEL = 1024
BATCH = 4
SEQ = 8192
DEPTH = 1
DEC_BATCH = 8
DEC_SEQ = 64
PAST_LEN = 2048

CHUNK = 64
CONV_W = 4
FFN_CONV_W = 3
EPS = 1e-6
GDN_HEADS = 8
GDN_DK = 128
GDN_DV = 128
GDN_QK = GDN_HEADS * GDN_DK
GDN_V = GDN_HEADS * GDN_DV
SSD_INNER = 2 * D_MODEL
SSD_HEADDIM = 64
SSD_HEADS = SSD_INNER // SSD_HEADDIM
SSD_STATE = 128
SSD_GROUPS = 8
SSD_BC = SSD_GROUPS * SSD_STATE
D_FF = 2816
CONV_CH = 2 * GDN_QK + GDN_V + SSD_INNER + 2 * SSD_BC
IN_COLS = CONV_CH + GDN_V + SSD_INNER + 2 * GDN_HEADS + SSD_HEADS + 2 * D_MODEL

kernel_name = 'hybrid_gdn_ssd_convffn_stream_step'


def _split(a, sizes):
    idx = np.cumsum(sizes)[:-1].tolist()
    return jnp.split(a, idx, axis=-1)


def _rms(xf):
    return xf * lax.rsqrt(jnp.mean(xf * xf, axis=-1, keepdims=True) + EPS)


def rmsnorm(x, w):
    return (_rms(x.astype(jnp.float32)) * w.astype(jnp.float32)).astype(x.dtype)


def l2norm(x):
    xf = x.astype(jnp.float32)
    return xf * lax.rsqrt(jnp.sum(xf * xf, axis=-1, keepdims=True) + EPS)


def causal_dwconv(u, buf, w, b):
    W = w.shape[0]
    T = u.shape[1]
    full = jnp.concatenate([buf.astype(u.dtype), u], axis=1)
    y = b + sum(full[:, i:i + T] * w[i] for i in range(W))
    return y, full[:, T:]


def to_chunks(a, blk):
    bsz, t = a.shape[:2]
    return jnp.moveaxis(a.reshape((bsz, t // blk, blk) + a.shape[2:]), 1, 0)


def from_chunks(a):
    a = jnp.moveaxis(a, 0, 1)
    return a.reshape((a.shape[0], a.shape[1] * a.shape[2]) + a.shape[3:])


def gated_delta_chunked(q, k, v, g, beta, s0):
    t = q.shape[1]
    blk = min(CHUNK, t)
    xs = tuple(to_chunks(a, blk) for a in (q, k, v, g, beta))
    causal = jnp.tril(jnp.ones((blk, blk), dtype=bool))
    strict = jnp.tril(jnp.ones((blk, blk), dtype=bool), -1)
    eye = jnp.eye(blk, dtype=jnp.float32)

    def step(s, inp):
        qc, kc, vc, gc, bc = inp
        gam = jnp.moveaxis(jnp.cumsum(gc, axis=1), 1, -1)
        dec = jnp.exp(jnp.where(causal, gam[..., :, None] - gam[..., None, :], -jnp.inf))
        bt = jnp.moveaxis(bc, 1, -1)
        qh = jnp.moveaxis(qc, 1, 2)
        kh = jnp.moveaxis(kc, 1, 2)
        vh = jnp.moveaxis(vc, 1, 2)
        kk = jnp.einsum('bhik,bhjk->bhij', kh, kh)
        lhs = jnp.where(strict, kk * dec * bt[..., :, None], 0.0) + eye
        rhs = jnp.concatenate([vh * bt[..., None], kh * (bt * jnp.exp(gam))[..., None]], axis=-1)
        sol = lax.linalg.triangular_solve(lhs, rhs, left_side=True, lower=True, unit_diagonal=True)
        w_v, w_k = sol[..., :GDN_DV], sol[..., GDN_DV:]
        u = w_v - jnp.einsum('bhik,bhkv->bhiv', w_k, s)
        qk = jnp.einsum('bhik,bhjk->bhij', qh, kh) * dec
        o = jnp.exp(gam)[..., None] * jnp.einsum('bhik,bhkv->bhiv', qh, s) + jnp.einsum('bhij,bhjv->bhiv', qk, u)
        last = gam[..., -1:]
        s = jnp.exp(last)[..., None] * s + jnp.einsum('bhik,bhiv->bhkv', kh * jnp.exp(last - gam)[..., None], u)
        return s, jnp.moveaxis(o, 2, 1)

    s_fin, outs = lax.scan(step, s0.astype(jnp.float32), xs)
    return from_chunks(outs), s_fin


def ssd_chunked(x, dt, a_neg, bm, cm, h0):
    bsz, t = x.shape[:2]
    blk = min(CHUNK, t)
    hpg = SSD_HEADS // SSD_GROUPS
    f32 = jnp.float32
    xs = (to_chunks(x.astype(f32).reshape(bsz, t, SSD_GROUPS, hpg, SSD_HEADDIM), blk),
          to_chunks(dt.reshape(bsz, t, SSD_GROUPS, hpg), blk),
          to_chunks(bm.astype(f32), blk),
          to_chunks(cm.astype(f32), blk))
    a = a_neg.reshape(SSD_GROUPS, hpg)
    causal = jnp.tril(jnp.ones((blk, blk), dtype=bool))

    def step(h, inp):
        xc, dtc, bc, cc = inp
        gam = jnp.cumsum(dtc * a, axis=1)
        gam_t = jnp.moveaxis(gam, 1, -1)
        dt_t = jnp.moveaxis(dtc, 1, -1)
        dec = jnp.exp(jnp.where(causal, gam_t[..., :, None] - gam_t[..., None, :], -jnp.inf))
        cb = jnp.einsum('bign,bjgn->bgij', cc, bc)
        w = cb[:, :, None] * dec * dt_t[..., None, :]
        y = jnp.einsum('bghij,bjghp->bighp', w, xc)
        y = y + jnp.einsum('bign,bghnp->bighp', cc, h) * jnp.exp(gam)[..., None]
        last = gam_t[..., -1:]
        wj = jnp.exp(last - gam_t) * dt_t
        h = jnp.exp(last)[..., None] * h + jnp.einsum('bjgn,bghj,bjghp->bghnp', bc, wj, xc)
        return h, y

    h0g = h0.astype(f32).reshape(bsz, SSD_GROUPS, hpg, SSD_STATE, SSD_HEADDIM)
    h_fin, ys = lax.scan(step, h0g, xs)
    y = from_chunks(ys).reshape(bsz, t, SSD_HEADS, SSD_HEADDIM)
    return y, h_fin.reshape(bsz, SSD_HEADS, SSD_STATE, SSD_HEADDIM)


def trunk_layer(x, c, conv_mix0, s_gdn0, s_ssd0, conv_ffn0,
                w_ada, b_ada, w_norm_mix, w_in, w_conv_mix, b_conv_mix,
                gdn_a_log, gdn_dt_bias, gdn_norm, ssd_a_log, ssd_dt_bias, ssd_d, ssd_norm,
                w_branch_gdn, w_branch_ssd, w_out,
                w_norm_ffn, w_ffn_up, w_ffn_conv, b_ffn_conv, w_ffn_down):
    f32 = jnp.float32
    bsz, t, _ = x.shape
    mod = (jax.nn.silu(c) @ w_ada + b_ada)[:, None, :]
    sh1, sc1, gt1, sh2, sc2, gt2 = jnp.split(mod, 6, axis=-1)

    h = rmsnorm(x, w_norm_mix) * (1 + sc1) + sh1
    proj = h @ w_in
    xconv, gdn_z, ssd_z, gdn_b, gdn_a, ssd_dt, merge = _split(
        proj, [CONV_CH, GDN_V, SSD_INNER, GDN_HEADS, GDN_HEADS, SSD_HEADS, 2 * D_MODEL])
    xconv, conv_mix_new = causal_dwconv(xconv, conv_mix0, w_conv_mix, b_conv_mix)
    xconv = jax.nn.silu(xconv)
    q, k, v, xs, bm, cm = _split(xconv, [GDN_QK, GDN_QK, GDN_V, SSD_INNER, SSD_BC, SSD_BC])

    q = l2norm(q.reshape(bsz, t, GDN_HEADS, GDN_DK)) * (GDN_DK ** -0.5)
    k = l2norm(k.reshape(bsz, t, GDN_HEADS, GDN_DK))
    v = v.reshape(bsz, t, GDN_HEADS, GDN_DV).astype(f32)
    beta = jax.nn.sigmoid(gdn_b.astype(f32))
    g = -jnp.exp(gdn_a_log.astype(f32)) * jax.nn.softplus(gdn_a.astype(f32) + gdn_dt_bias.astype(f32))
    o_gdn, s_gdn_new = gated_delta_chunked(q, k, v, g, beta, s_gdn0)
    zg = gdn_z.reshape(bsz, t, GDN_HEADS, GDN_DV).astype(f32)
    o_gdn = (_rms(o_gdn) * gdn_norm.astype(f32) * jax.nn.silu(zg)).reshape(bsz, t, GDN_V).astype(x.dtype)

    xs = xs.reshape(bsz, t, SSD_HEADS, SSD_HEADDIM)
    bm = bm.reshape(bsz, t, SSD_GROUPS, SSD_STATE)
    cm = cm.reshape(bsz, t, SSD_GROUPS, SSD_STATE)
    dt = jax.nn.softplus(ssd_dt.astype(f32) + ssd_dt_bias.astype(f32))
    a_neg = -jnp.exp(ssd_a_log.astype(f32))
    y_ssd, s_ssd_new = ssd_chunked(xs, dt, a_neg, bm, cm, s_ssd0)
    y_ssd = y_ssd + ssd_d.astype(f32)[:, None] * xs.astype(f32)
    gsz = SSD_INNER // SSD_GROUPS
    yz = (y_ssd.reshape(bsz, t, SSD_INNER) * jax.nn.silu(ssd_z.astype(f32))).reshape(bsz, t, SSD_GROUPS, gsz)
    y_ssd = (_rms(yz) * ssd_norm.astype(f32).reshape(SSD_GROUPS, gsz)).reshape(bsz, t, SSD_INNER).astype(x.dtype)

    g_gdn, g_ssd = jnp.split(jax.nn.sigmoid(merge.astype(f32)).astype(x.dtype), 2, axis=-1)
    mixed = g_gdn * (o_gdn @ w_branch_gdn) + g_ssd * (y_ssd @ w_branch_ssd)
    x = x + gt1 * (mixed @ w_out)

    h2 = rmsnorm(x, w_norm_ffn) * (1 + sc2) + sh2
    u = h2 @ w_ffn_up
    u, conv_ffn_new = causal_dwconv(u, conv_ffn0, w_ffn_conv, b_ffn_conv)
    ua, uv = jnp.split(u, 2, axis=-1)
    x = x + gt2 * ((jax.nn.silu(ua) * uv) @ w_ffn_down)
    return (x, conv_mix_new, s_gdn_new.astype(x.dtype), s_ssd_new.astype(x.dtype), conv_ffn_new)


def setup_inputs(seed: int = 0) -> dict:
    key = jax.random.key(seed)
    ks = jax.random.split(key, 32)
    f32 = jnp.float32
    L = DEPTH

    def nrm(k, shape, scale):
        return scale * jax.random.normal(k, shape, f32)

    def inv_softplus_dt(k, shape):
        dt = jnp.exp(jax.random.uniform(k, shape, f32, minval=math.log(1e-3), maxval=math.log(1e-1)))
        return dt + jnp.log(-jnp.expm1(-dt))

    return {
        'x_prompt': nrm(ks[0], (BATCH, SEQ, D_MODEL), 1.0),
        'x_sample': nrm(ks[1], (DEC_BATCH, DEC_SEQ, D_MODEL), 1.0),
        'state_conv_mix': nrm(ks[2], (L, DEC_BATCH, CONV_W - 1, CONV_CH), 1.0),
        'state_gdn': nrm(ks[3], (L, DEC_BATCH, GDN_HEADS, GDN_DK, GDN_DV), 0.1),
        'state_ssd': nrm(ks[4], (L, DEC_BATCH, SSD_HEADS, SSD_STATE, SSD_HEADDIM), 0.1),
        'state_conv_ffn': nrm(ks[5], (L, DEC_BATCH, FFN_CONV_W - 1, 2 * D_FF), 1.0),
        'c_prompt': nrm(ks[6], (BATCH, D_MODEL), 1.0),
        'c_sample': nrm(ks[7], (DEC_BATCH, D_MODEL), 1.0),
        'w_ada': nrm(ks[8], (L, D_MODEL, 6 * D_MODEL), D_MODEL ** -0.5),
        'b_ada': nrm(ks[9], (L, 6 * D_MODEL), 0.02),
        'w_norm_mix': 1.0 + nrm(ks[10], (L, D_MODEL), 0.02),
        'w_in': nrm(ks[11], (L, D_MODEL, IN_COLS), D_MODEL ** -0.5),
        'w_conv_mix': nrm(ks[12], (L, CONV_W, CONV_CH), CONV_W ** -0.5),
        'b_conv_mix': nrm(ks[13], (L, CONV_CH), 0.02),
        'gdn_a_log': jnp.log(jax.random.uniform(ks[14], (L, GDN_HEADS), f32, minval=1.0, maxval=16.0)),
        'gdn_dt_bias': inv_softplus_dt(ks[15], (L, GDN_HEADS)),
        'gdn_norm': 1.0 + nrm(ks[16], (L, GDN_DV), 0.02),
        'ssd_a_log': jnp.log(jax.random.uniform(ks[17], (L, SSD_HEADS), f32, minval=1.0, maxval=16.0)),
        'ssd_dt_bias': inv_softplus_dt(ks[18], (L, SSD_HEADS)),
        'ssd_d': 1.0 + nrm(ks[19], (L, SSD_HEADS), 0.02),
        'ssd_norm': 1.0 + nrm(ks[20], (L, SSD_INNER), 0.02),
        'w_branch_gdn': nrm(ks[21], (L, GDN_V, D_MODEL), GDN_V ** -0.5),
        'w_branch_ssd': nrm(ks[22], (L, SSD_INNER, D_MODEL), SSD_INNER ** -0.5),
        'w_out': nrm(ks[23], (L, D_MODEL, D_MODEL), D_MODEL ** -0.5),
        'w_norm_ffn': 1.0 + nrm(ks[24], (L, D_MODEL), 0.02),
        'w_ffn_up': nrm(ks[25], (L, D_MODEL, 2 * D_FF), D_MODEL ** -0.5),
        'w_ffn_conv': nrm(ks[26], (L, FFN_CONV_W, 2 * D_FF), FFN_CONV_W ** -0.5),
        'b_ffn_conv': nrm(ks[27], (L, 2 * D_FF), 0.02),
        'w_ffn_down': nrm(ks[28], (L, D_FF, D_MODEL), D_FF ** -0.5),
        'w_norm_final': 1.0 + nrm(ks[29], (D_MODEL,), 0.02),
    }


def reference(x_prompt, x_sample, state_conv_mix, state_gdn, state_ssd, state_conv_ffn,
              c_prompt, c_sample,
              w_ada, b_ada, w_norm_mix, w_in, w_conv_mix, b_conv_mix,
              gdn_a_log, gdn_dt_bias, gdn_norm, ssd_a_log, ssd_dt_bias, ssd_d, ssd_norm,
              w_branch_gdn, w_branch_ssd, w_out,
              w_norm_ffn, w_ffn_up, w_ffn_conv, b_ffn_conv, w_ffn_down, w_norm_final):
    nb = x_prompt.shape[0]
    dtype = x_prompt.dtype
    xp, xsm = x_prompt, x_sample
    cm_p, gd_p, ss_p, cf_p = [], [], [], []
    cm_s, gd_s, ss_s, cf_s = [], [], [], []
    for l in range(DEPTH):
        params = (w_ada[l], b_ada[l], w_norm_mix[l], w_in[l], w_conv_mix[l], b_conv_mix[l],
                  gdn_a_log[l], gdn_dt_bias[l], gdn_norm[l], ssd_a_log[l], ssd_dt_bias[l], ssd_d[l], ssd_norm[l],
                  w_branch_gdn[l], w_branch_ssd[l], w_out[l],
                  w_norm_ffn[l], w_ffn_up[l], w_ffn_conv[l], b_ffn_conv[l], w_ffn_down[l])
        z_conv = jnp.zeros((nb, CONV_W - 1, CONV_CH), dtype)
        z_gdn = jnp.zeros((nb, GDN_HEADS, GDN_DK, GDN_DV), dtype)
        z_ssd = jnp.zeros((nb, SSD_HEADS, SSD_STATE, SSD_HEADDIM), dtype)
        z_ffn = jnp.zeros((nb, FFN_CONV_W - 1, 2 * D_FF), dtype)
        xp, a1, a2, a3, a4 = trunk_layer(xp, c_prompt, z_conv, z_gdn, z_ssd, z_ffn, *params)
        xsm, b1, b2, b3, b4 = trunk_layer(xsm, c_sample, state_conv_mix[l], state_gdn[l], state_ssd[l],
                                          state_conv_ffn[l], *params)
        cm_p.append(a1); gd_p.append(a2); ss_p.append(a3); cf_p.append(a4)
        cm_s.append(b1); gd_s.append(b2); ss_s.append(b3); cf_s.append(b4)
    y_prompt = rmsnorm(xp, w_norm_final)
    y_sample = rmsnorm(xsm, w_norm_final)
    return (y_prompt, y_sample,
            jnp.stack(cm_p), jnp.stack(gd_p), jnp.stack(ss_p), jnp.stack(cf_p),
            jnp.stack(cm_s), jnp.stack(gd_s), jnp.stack(ss_s), jnp.stack(cf_s))
```

```python
import functools

import jax
import jax.numpy as jnp
from jax import lax
from jax.experimental import pallas as pl
from jax.experimental.pallas import tpu as pltpu

F32 = jnp.float32
BF16 = jnp.bfloat16
HIGHEST = lax.Precision.HIGHEST

D_MODEL = 1024
CHUNK = 64
EPS = 1e-6
CONV_W = 4
FFN_CONV_W = 3
GDN_HEADS = 8
GDN_DK = 128
GDN_DV = 128
GDN_QK = GDN_HEADS * GDN_DK
GDN_V = GDN_HEADS * GDN_DV
SSD_INNER = 2 * D_MODEL
SSD_HEADDIM = 64
SSD_HEADS = SSD_INNER // SSD_HEADDIM
SSD_STATE = 128
SSD_GROUPS = 8
SSD_HPG = SSD_HEADS // SSD_GROUPS
SSD_GSZ = SSD_INNER // SSD_GROUPS
SSD_BC = SSD_GROUPS * SSD_STATE
D_FF = 2816
CONV_CH = 2 * GDN_QK + GDN_V + SSD_INNER + 2 * SSD_BC
RAW_CH = GDN_V + SSD_INNER + 2 * D_MODEL
SMALL_CH = 2 * GDN_HEADS + SSD_HEADS
LANES = 128
SUBLANES = 8
LANE_GDN_B = 0
LANE_GDN_A = GDN_HEADS
LANE_SSD_DT = 2 * GDN_HEADS
VMEM_LIMIT_BYTES = 56 * 1024 * 1024


def _sigmoid(x):
    return 1.0 / (1.0 + jnp.exp(-x))


def _silu(x):
    return x * _sigmoid(x)


def _softplus(x):
    return jnp.maximum(x, 0.0) + jnp.log(1.0 + jnp.exp(-jnp.abs(x)))


def _dot(a, b, precision=None):
    return jnp.dot(a, b, preferred_element_type=F32, precision=precision)


def _dot_nt(a, b):
    return lax.dot_general(a, b, (((1,), (1,)), ((), ())), preferred_element_type=F32)


def _dot_tn(a, b):
    return lax.dot_general(a, b, (((0,), (0,)), ((), ())), preferred_element_type=F32)


def _const_spec(shape):
    zeros = (0,) * len(shape)
    return pl.BlockSpec(shape, lambda *_: zeros)


def _mod_kernel(c_ref, w_ref, b_ref, o_ref):
    o_ref[...] = _dot(_silu(c_ref[...]), w_ref[...], HIGHEST) + b_ref[...]


def _mod_call(c_pad, w_ada, b_ada):
    rows = c_pad.shape[0]
    n = w_ada.shape[1]
    tn = 512
    return pl.pallas_call(
        _mod_kernel,
        grid=(n // tn,),
        in_specs=[pl.BlockSpec((rows, D_MODEL), lambda j: (0, 0)),
                  pl.BlockSpec((D_MODEL, tn), lambda j: (0, j)),
                  pl.BlockSpec((1, tn), lambda j: (0, j))],
        out_specs=pl.BlockSpec((rows, tn), lambda j: (0, j)),
        out_shape=jax.ShapeDtypeStruct((rows, n), F32),
        name="adaln_mod",
    )(c_pad, w_ada, b_ada.reshape(1, n))


def _inproj_kernel(x_ref, mod_ref, wn_ref, w_ref, cw_ref, cb_ref, st_ref,
                   xc_ref, raw_ref, sm_ref, cst_ref,
                   h_scr, tail_scr, stage_scr, *, nseq, tl, n_conv, n_raw):
    m = pl.program_id(1)
    n = pl.program_id(2)

    @pl.when(n == 0)
    def _():
        for s in range(nseq):
            x = x_ref[s]
            ms = jnp.mean(x * x, axis=-1, keepdims=True)
            md = mod_ref[s]
            h = (x * lax.rsqrt(ms + EPS) * wn_ref[...]) * (1.0 + md[1:2]) + md[0:1]
            h_scr[s * tl:(s + 1) * tl, :] = h.astype(BF16)

    acc = _dot(h_scr[...], w_ref[...])

    @pl.when(n < n_conv)
    def _():
        for s in range(nseq):
            @pl.when(m == 0)
            def _():
                tail_scr[n, s] = st_ref[s]
            stage_scr[0:SUBLANES] = tail_scr[n, s]
            stage_scr[SUBLANES:SUBLANES + tl] = acc[s * tl:(s + 1) * tl]
            y = cb_ref[...] + cw_ref[3:4] * stage_scr[8:8 + tl]
            y = y + cw_ref[2:3] * stage_scr[7:7 + tl]
            y = y + cw_ref[1:2] * stage_scr[6:6 + tl]
            y = y + cw_ref[0:1] * stage_scr[5:5 + tl]
            xc_ref[s] = _silu(y)
            new_tail = stage_scr[tl:tl + SUBLANES]
            tail_scr[n, s] = new_tail
            cst_ref[s, 0] = new_tail

    @pl.when(jnp.logical_and(n >= n_conv, n < n_conv + n_raw))
    def _():
        for s in range(nseq):
            raw_ref[s] = acc[s * tl:(s + 1) * tl]

    @pl.when(n == n_conv + n_raw)
    def _():
        for s in range(nseq):
            sm_ref[s] = acc[s * tl:(s + 1) * tl, 0:LANES]


def _inproj_call(x, mod8, w_norm, w_all, cw8, cb, st8, *, nseq, tl, tn):
    b, t, _ = x.shape
    n_conv = CONV_CH // tn
    n_raw = RAW_CH // tn
    nt = n_conv + n_raw + 1
    assert w_all.shape == (D_MODEL, nt * tn)
    grid = (b // nseq, t // tl, nt)
    conv_idx = lambda n: jnp.minimum(n, n_conv - 1)
    raw_idx = lambda n: jnp.clip(n - n_conv, 0, n_raw - 1)
    kern = functools.partial(_inproj_kernel, nseq=nseq, tl=tl, n_conv=n_conv, n_raw=n_raw)
    return pl.pallas_call(
        kern,
        grid=grid,
        in_specs=[
            pl.BlockSpec((nseq, tl, D_MODEL), lambda i, m, n: (i, m, 0)),
            pl.BlockSpec((nseq, SUBLANES, D_MODEL), lambda i, m, n: (i, 0, 0)),
            pl.BlockSpec((1, D_MODEL), lambda i, m, n: (0, 0)),
            pl.BlockSpec((D_MODEL, tn), lambda i, m, n: (0, n)),
            pl.BlockSpec((SUBLANES, tn), lambda i, m, n: (0, conv_idx(n))),
            pl.BlockSpec((1, tn), lambda i, m, n: (0, conv_idx(n))),
            pl.BlockSpec((nseq, SUBLANES, tn), lambda i, m, n: (i, 0, conv_idx(n))),
        ],
        out_specs=[
            pl.BlockSpec((nseq, tl, tn), lambda i, m, n: (i, m, conv_idx(n))),
            pl.BlockSpec((nseq, tl, tn), lambda i, m, n: (i, m, raw_idx(n))),
            pl.BlockSpec((nseq, tl, LANES), lambda i, m, n: (i, m, 0)),
            pl.BlockSpec((nseq, 1, SUBLANES, tn), lambda i, m, n: (i, m, 0, conv_idx(n))),
        ],
        out_shape=[
            jax.ShapeDtypeStruct((b, t, CONV_CH), F32),
            jax.ShapeDtypeStruct((b, t, RAW_CH), F32),
            jax.ShapeDtypeStruct((b, t, LANES), F32),
            jax.ShapeDtypeStruct((b, t // tl, SUBLANES, CONV_CH), F32),
        ],
        scratch_shapes=[
            pltpu.VMEM((nseq * tl, D_MODEL), BF16),
            pltpu.VMEM((n_conv, nseq, SUBLANES, tn), F32),
            pltpu.VMEM((tl + SUBLANES, tn), F32),
        ],
        compiler_params=pltpu.CompilerParams(
            dimension_semantics=("arbitrary", "arbitrary", "arbitrary"),
            vmem_limit_bytes=VMEM_LIMIT_BYTES),
        name="inproj_conv",
    )(x, mod8, w_norm, w_all, cw8, cb, st8)


def _chunk_masks():
    row = lax.broadcasted_iota(jnp.int32, (CHUNK, CHUNK), 0)
    col = lax.broadcasted_iota(jnp.int32, (CHUNK, CHUNK), 1)
    return row >= col, row > col, row == col


def _gdn_kernel(q_ref, k_ref, v_ref, z_ref, sm_ref, s0_ref, aneg_ref, dtb_ref, gn_ref,
                o_ref, sout_ref, s_scr, *, nchunk):
    t = pl.program_id(1)

    @pl.when(t == 0)
    def _():
        s_scr[...] = s0_ref[0]

    causal, strict, diag = _chunk_masks()
    tril = causal.astype(F32)
    eye = diag.astype(F32)
    qscale = GDN_DK ** -0.5

    def chunk(c, carry):
        r0 = pl.multiple_of(c * CHUNK, CHUNK)
        rows = pl.ds(r0, CHUNK)
        sm = sm_ref[0, rows, :]
        beta_all = _sigmoid(sm)
        g_all = aneg_ref[...] * _softplus(sm + dtb_ref[...])
        gam_all = _dot(tril, g_all, HIGHEST)
        gam_t = gam_all.T
        egam_all = jnp.exp(gam_all)
        last_all = gam_all[CHUNK - 1:CHUNK, :]
        elast_all = jnp.exp(last_all)
        erem_all = jnp.exp(last_all - gam_all)
        for h in range(GDN_HEADS):
            cols = slice(h * GDN_DK, (h + 1) * GDN_DK)
            lb = LANE_GDN_B + h
            lg = LANE_GDN_A + h
            b_i = beta_all[:, lb:lb + 1]
            gam_i = gam_all[:, lg:lg + 1]
            gam_j = gam_t[lg:lg + 1, :]
            egam_i = egam_all[:, lg:lg + 1]
            qr = q_ref[0, rows, cols]
            kr = k_ref[0, rows, cols]
            vh = v_ref[0, rows, cols]
            qh = qr * (lax.rsqrt(jnp.sum(qr * qr, axis=-1, keepdims=True) + EPS) * qscale)
            kh = kr * lax.rsqrt(jnp.sum(kr * kr, axis=-1, keepdims=True) + EPS)
            qb = qh.astype(BF16)
            kb = kh.astype(BF16)
            dec = jnp.exp(jnp.where(causal, gam_i - gam_j, -jnp.inf))
            kk = _dot_nt(kb, kb)
            a = jnp.where(strict, kk * dec * b_i, 0.0)
            y = -a
            p = eye + y
            for _ in range(5):
                y = _dot(y, y, HIGHEST)
                p = p + _dot(p, y, HIGHEST)
            w_v = _dot(p, vh * b_i, HIGHEST)
            w_k = _dot(p, kh * (b_i * egam_i), HIGHEST)
            s = s_scr[h]
            sb = s.astype(BF16)
            u = w_v - _dot(w_k.astype(BF16), sb)
            ub = u.astype(BF16)
            qk = _dot_nt(qb, kb) * dec
            o = egam_i * _dot(qb, sb) + _dot(qk.astype(BF16), ub)
            kdec = (kh * erem_all[:, lg:lg + 1]).astype(BF16)
            s_scr[h] = elast_all[:, lg:lg + 1] * s + _dot_tn(kdec, ub)
            zh = z_ref[0, rows, cols]
            on = o * lax.rsqrt(jnp.mean(o * o, axis=-1, keepdims=True) + EPS) * gn_ref[...] * _silu(zh)
            o_ref[0, rows, cols] = on.astype(o_ref.dtype)
        return carry

    lax.fori_loop(0, nchunk, chunk, 0)

    @pl.when(t == pl.num_programs(1) - 1)
    def _():
        sout_ref[0] = s_scr[...]


def _gdn_call(xc, raw, small, s0, aneg_l, dtb_l, gnorm, *, tc):
    b, t, _ = xc.shape
    kern = functools.partial(_gdn_kernel, nchunk=tc // CHUNK)
    wide = lambda j: pl.BlockSpec((1, tc, GDN_QK), lambda i, m: (i, m, j))
    state = pl.BlockSpec((1, GDN_HEADS, GDN_DK, GDN_DV), lambda i, m: (i, 0, 0, 0))
    return pl.pallas_call(
        kern,
        grid=(b, t // tc),
        in_specs=[wide(0), wide(1), wide(2), wide(0),
                  pl.BlockSpec((1, tc, LANES), lambda i, m: (i, m, 0)),
                  state,
                  _const_spec((1, LANES)), _const_spec((1, LANES)), _const_spec((1, GDN_DV))],
        out_specs=[pl.BlockSpec((1, tc, GDN_V), lambda i, m: (i, m, 0)), state],
        out_shape=[jax.ShapeDtypeStruct((b, t, GDN_V), BF16),
                   jax.ShapeDtypeStruct((b, GDN_HEADS, GDN_DK, GDN_DV), F32)],
        scratch_shapes=[pltpu.VMEM((GDN_HEADS, GDN_DK, GDN_DV), F32)],
        compiler_params=pltpu.CompilerParams(
            dimension_semantics=("arbitrary", "arbitrary"),
            vmem_limit_bytes=VMEM_LIMIT_BYTES),
        name="gdn_scan",
    )(xc, xc, xc, raw, small, s0, aneg_l, dtb_l, gnorm)


def _ssd_kernel(xlo_ref, xhi_ref, bm_ref, cm_ref, zlo_ref, zhi_ref, sm_ref, h0_ref,
                aneg_ref, dtb_ref, dexp_ref, nrm_ref,
                y_ref, hout_ref, h_scr, *, nchunk):
    t = pl.program_id(1)

    @pl.when(t == 0)
    def _():
        h_scr[...] = h0_ref[0]

    causal, _, _ = _chunk_masks()
    tril = causal.astype(F32)
    head_of_lane = lax.broadcasted_iota(jnp.int32, (1, SSD_GSZ), 1) // SSD_HEADDIM
    half = SSD_GROUPS // 2

    def expand(cols_of_head):
        out = jnp.broadcast_to(cols_of_head[0], (cols_of_head[0].shape[0], SSD_GSZ))
        for j in range(1, SSD_HPG):
            out = jnp.where(head_of_lane == j, cols_of_head[j], out)
        return out

    def chunk(c, carry):
        r0 = pl.multiple_of(c * CHUNK, CHUNK)
        rows = pl.ds(r0, CHUNK)
        sm = sm_ref[0, rows, :]
        dt_all = _softplus(sm + dtb_ref[...])
        gam_all = _dot(tril, dt_all * aneg_ref[...], HIGHEST)
        gam_t = gam_all.T
        dt_t = dt_all.T
        egam_all = jnp.exp(gam_all)
        last_all = gam_all[CHUNK - 1:CHUNK, :]
        elast_all = jnp.exp(last_all)
        wj_all = jnp.exp(last_all - gam_all) * dt_all
        for g in range(SSD_GROUPS):
            x_ref = xlo_ref if g < half else xhi_ref
            z_ref = zlo_ref if g < half else zhi_ref
            gcols = slice((g % half) * SSD_GSZ, (g % half + 1) * SSD_GSZ)
            ncols = slice(g * SSD_STATE, (g + 1) * SSD_STATE)
            ocols = slice(g * SSD_GSZ, (g + 1) * SSD_GSZ)
            xg = x_ref[0, rows, gcols]
            xb = xg.astype(BF16)
            bmb = bm_ref[0, rows, ncols].astype(BF16)
            cmb = cm_ref[0, rows, ncols].astype(BF16)
            cb = _dot_nt(cmb, bmb)
            lanes = [LANE_SSD_DT + g * SSD_HPG + j for j in range(SSD_HPG)]
            ws = []
            for l in lanes:
                dec = jnp.exp(jnp.where(causal, gam_all[:, l:l + 1] - gam_t[l:l + 1, :], -jnp.inf))
                ws.append((cb * dec * dt_t[l:l + 1, :]).astype(BF16))
            r = _dot(jnp.concatenate(ws, axis=0), xb)
            y = r[0:CHUNK]
            for j in range(1, SSD_HPG):
                y = jnp.where(head_of_lane == j, r[j * CHUNK:(j + 1) * CHUNK], y)
            egam = expand([egam_all[:, l:l + 1] for l in lanes])
            wj = expand([wj_all[:, l:l + 1] for l in lanes])
            elast = expand([elast_all[:, l:l + 1] for l in lanes])
            hg = h_scr[g]
            y = y + _dot(cmb, hg.astype(BF16)) * egam
            h_scr[g] = elast * hg + _dot_tn(bmb, (xg * wj).astype(BF16))
            y = y + dexp_ref[:, ocols] * xg
            yz = y * _silu(z_ref[0, rows, gcols])
            yn = yz * lax.rsqrt(jnp.mean(yz * yz, axis=-1, keepdims=True) + EPS) * nrm_ref[:, ocols]
            y_ref[0, rows, ocols] = yn.astype(y_ref.dtype)
        return carry

    lax.fori_loop(0, nchunk, chunk, 0)

    @pl.when(t == pl.num_programs(1) - 1)
    def _():
        hout_ref[0] = h_scr[...]


def _ssd_call(xc, raw, small, h0g, aneg_l, dtb_l, dexp, nrm, *, tc):
    b, t, _ = xc.shape
    kern = functools.partial(_ssd_kernel, nchunk=tc // CHUNK)
    wide = lambda j: pl.BlockSpec((1, tc, 1024), lambda i, m: (i, m, j))
    state = pl.BlockSpec((1, SSD_GROUPS, SSD_STATE, SSD_GSZ), lambda i, m: (i, 0, 0, 0))
    return pl.pallas_call(
        kern,
        grid=(b, t // tc),
        in_specs=[wide(3), wide(4), wide(5), wide(6), wide(1), wide(2),
                  pl.BlockSpec((1, tc, LANES), lambda i, m: (i, m, 0)),
                  state,
                  _const_spec((1, LANES)), _const_spec((1, LANES)),
                  _const_spec((1, SSD_INNER)), _const_spec((1, SSD_INNER))],
        out_specs=[pl.BlockSpec((1, tc, SSD_INNER), lambda i, m: (i, m, 0)), state],
        out_shape=[jax.ShapeDtypeStruct((b, t, SSD_INNER), BF16),
                   jax.ShapeDtypeStruct((b, SSD_GROUPS, SSD_STATE, SSD_GSZ), F32)],
        scratch_shapes=[pltpu.VMEM((SSD_GROUPS, SSD_STATE, SSD_GSZ), F32)],
        compiler_params=pltpu.CompilerParams(
            dimension_semantics=("arbitrary", "arbitrary"),
            vmem_limit_bytes=VMEM_LIMIT_BYTES),
        name="ssd_scan",
    )(xc, xc, xc, xc, raw, raw, small, h0g, aneg_l, dtb_l, dexp, nrm)


def _mlp_kernel(x_ref, og_ref, ys_ref, ga_ref, gb_ref, mod_ref,
                wbg_ref, wbs_ref, wo_ref, wn2_ref, wup_ref, cw_ref, cb_ref, wdn_ref, wnf_ref, st_ref,
                out_ref, cst_ref, tail_scr, sa_scr, sv_scr, *, nseq, tl, cf):
    m = pl.program_id(1)
    for s in range(nseq):
        @pl.when(m == 0)
        def _():
            tail_scr[s] = st_ref[s]

        md = mod_ref[s]
        mixed = (_sigmoid(ga_ref[s]) * _dot(og_ref[s], wbg_ref[...])
                 + _sigmoid(gb_ref[s]) * _dot(ys_ref[s], wbs_ref[...]))
        x1 = x_ref[s] + md[2:3] * _dot(mixed.astype(BF16), wo_ref[...])
        ms = jnp.mean(x1 * x1, axis=-1, keepdims=True)
        h2 = ((x1 * lax.rsqrt(ms + EPS) * wn2_ref[...]) * (1.0 + md[4:5]) + md[3:4]).astype(BF16)

        def conv(stage, u, cols):
            stage[0:SUBLANES] = tail_scr[s, :, cols]
            stage[SUBLANES:SUBLANES + tl] = u
            y = cb_ref[:, cols] + cw_ref[2:3, cols] * stage[8:8 + tl]
            y = y + cw_ref[1:2, cols] * stage[7:7 + tl]
            y = y + cw_ref[0:1, cols] * stage[6:6 + tl]
            tail_scr[s, :, cols] = stage[tl:tl + SUBLANES]
            return y

        acc = jnp.zeros((tl, D_MODEL), F32)
        for j in range(D_FF // cf):
            ca = slice(j * cf, (j + 1) * cf)
            cv = slice(D_FF + j * cf, D_FF + (j + 1) * cf)
            ua = conv(sa_scr, _dot(h2, wup_ref[:, ca]), ca)
            uv = conv(sv_scr, _dot(h2, wup_ref[:, cv]), cv)
            act = (_silu(ua) * uv).astype(BF16)
            acc = acc + _dot(act, wdn_ref[ca, :])
        x2 = x1 + md[5:6] * acc
        ms2 = jnp.mean(x2 * x2, axis=-1, keepdims=True)
        out_ref[s] = x2 * lax.rsqrt(ms2 + EPS) * wnf_ref[...]
        cst_ref[s] = tail_scr[s]


def _mlp_call(x, og, ys, raw, mod8, wbg, wbs, wo, wn2, wup, cw8, cb, wdn, wnf, st8, *, nseq, tl, cf):
    b, t, _ = x.shape
    kern = functools.partial(_mlp_kernel, nseq=nseq, tl=tl, cf=cf)
    tok = lambda width, j: pl.BlockSpec((nseq, tl, width), lambda i, m: (i, m, j))
    per_seq = lambda width: pl.BlockSpec((nseq, SUBLANES, width), lambda i, m: (i, 0, 0))

    def resident(shape):
        zeros = (0,) * len(shape)
        return pl.BlockSpec(shape, lambda i, m: zeros, pipeline_mode=pl.Buffered(1))

    return pl.pallas_call(
        kern,
        grid=(b // nseq, t // tl),
        in_specs=[tok(D_MODEL, 0), tok(GDN_V, 0), tok(SSD_INNER, 0), tok(D_MODEL, 3), tok(D_MODEL, 4),
                  per_seq(D_MODEL),
                  resident((GDN_V, D_MODEL)), resident((SSD_INNER, D_MODEL)), resident((D_MODEL, D_MODEL)),
                  resident((1, D_MODEL)), resident((D_MODEL, 2 * D_FF)),
                  resident((SUBLANES, 2 * D_FF)), resident((1, 2 * D_FF)),
                  resident((D_FF, D_MODEL)), resident((1, D_MODEL)),
                  per_seq(2 * D_FF)],
        out_specs=[tok(D_MODEL, 0), per_seq(2 * D_FF)],
        out_shape=[jax.ShapeDtypeStruct((b, t, D_MODEL), F32),
                   jax.ShapeDtypeStruct((b, SUBLANES, 2 * D_FF), F32)],
        scratch_shapes=[pltpu.VMEM((nseq, SUBLANES, 2 * D_FF), F32),
                        pltpu.VMEM((tl + SUBLANES, cf), F32),
                        pltpu.VMEM((tl + SUBLANES, cf), F32)],
        compiler_params=pltpu.CompilerParams(
            dimension_semantics=("arbitrary", "arbitrary"),
            vmem_limit_bytes=VMEM_LIMIT_BYTES),
        name="merge_ffn",
    )(x, og, ys, raw, raw, mod8, wbg, wbs, wo, wn2, wup, cw8, cb, wdn, wnf, st8)


def _lane_vec(values, offset):
    return jnp.zeros((1, LANES), F32).at[0, offset:offset + values.shape[0]].set(values.astype(F32))


def _prep_params(w_norm_mix, w_in, w_conv_mix, b_conv_mix, gdn_a_log, gdn_dt_bias, gdn_norm,
                 ssd_a_log, ssd_dt_bias, ssd_d, ssd_norm, w_branch_gdn, w_branch_ssd, w_out,
                 w_norm_ffn, w_ffn_up, w_ffn_conv, b_ffn_conv, w_ffn_down, w_norm_final, tn):
    o_z = CONV_CH
    o_small = CONV_CH + GDN_V + SSD_INNER
    o_merge = o_small + SMALL_CH
    w_small = jnp.pad(w_in[:, o_small:o_merge], ((0, 0), (0, tn - SMALL_CH)))
    w_all = jnp.concatenate([w_in[:, :o_small], w_in[:, o_merge:], w_small], axis=1).astype(BF16)
    del o_z
    return dict(
        w_norm_mix=w_norm_mix.reshape(1, D_MODEL),
        w_all=w_all,
        cw_mix=jnp.pad(w_conv_mix, ((0, SUBLANES - CONV_W), (0, 0))),
        cb_mix=b_conv_mix.reshape(1, CONV_CH),
        gdn_aneg=_lane_vec(-jnp.exp(gdn_a_log.astype(F32)), LANE_GDN_A),
        gdn_dtb=_lane_vec(gdn_dt_bias, LANE_GDN_A),
        gdn_norm=gdn_norm.reshape(1, GDN_DV).astype(F32),
        ssd_aneg=_lane_vec(-jnp.exp(ssd_a_log.astype(F32)), LANE_SSD_DT),
        ssd_dtb=_lane_vec(ssd_dt_bias, LANE_SSD_DT),
        ssd_dexp=jnp.repeat(ssd_d.astype(F32), SSD_HEADDIM).reshape(1, SSD_INNER),
        ssd_norm=ssd_norm.reshape(1, SSD_INNER).astype(F32),
        wbg=w_branch_gdn.astype(BF16), wbs=w_branch_ssd.astype(BF16), wo=w_out.astype(BF16),
        w_norm_ffn=w_norm_ffn.reshape(1, D_MODEL),
        wup=w_ffn_up.astype(BF16),
        cw_ffn=jnp.pad(w_ffn_conv, ((0, SUBLANES - FFN_CONV_W), (0, 0))),
        cb_ffn=b_ffn_conv.reshape(1, 2 * D_FF),
        wdn=w_ffn_down.astype(BF16),
        w_norm_final=w_norm_final.reshape(1, D_MODEL),
    )


def _tiles(b, t):
    if t <= CHUNK:
        return dict(nseq=b, tl_in=t, tl_mlp=t, tc=t)
    tl_in = min(t, 1024)
    tl_mlp = min(t, 256)
    tc = min(t, 256)
    return dict(nseq=1, tl_in=tl_in, tl_mlp=tl_mlp, tc=tc)


def _trunk(x, mod, conv_mix0, s_gdn0, s_ssd0, conv_ffn0, p, tn, cf):
    b, t, _ = x.shape
    tl = _tiles(b, t)
    mod8 = jnp.pad(mod.reshape(b, 6, D_MODEL), ((0, 0), (0, SUBLANES - 6), (0, 0)))
    st_mix = jnp.pad(conv_mix0, ((0, 0), (SUBLANES - (CONV_W - 1), 0), (0, 0)))
    xc, raw, small, cst_mix = _inproj_call(
        x, mod8, p["w_norm_mix"], p["w_all"], p["cw_mix"], p["cb_mix"], st_mix,
        nseq=tl["nseq"], tl=tl["tl_in"], tn=tn)
    og, s_gdn = _gdn_call(xc, raw, small, s_gdn0, p["gdn_aneg"], p["gdn_dtb"], p["gdn_norm"], tc=tl["tc"])
    h0g = s_ssd0.reshape(b, SSD_GROUPS, SSD_HPG, SSD_STATE, SSD_HEADDIM)
    h0g = jnp.swapaxes(h0g, 2, 3).reshape(b, SSD_GROUPS, SSD_STATE, SSD_GSZ)
    ys, hg = _ssd_call(xc, raw, small, h0g, p["ssd_aneg"], p["ssd_dtb"], p["ssd_dexp"], p["ssd_norm"],
                       tc=tl["tc"])
    s_ssd = jnp.swapaxes(hg.reshape(b, SSD_GROUPS, SSD_STATE, SSD_HPG, SSD_HEADDIM), 2, 3)
    s_ssd = s_ssd.reshape(b, SSD_HEADS, SSD_STATE, SSD_HEADDIM)
    st_ffn = jnp.pad(conv_ffn0, ((0, 0), (SUBLANES - (FFN_CONV_W - 1), 0), (0, 0)))
    y, cst_ffn = _mlp_call(
        x, og, ys, raw, mod8, p["wbg"], p["wbs"], p["wo"], p["w_norm_ffn"], p["wup"],
        p["cw_ffn"], p["cb_ffn"], p["wdn"], p["w_norm_final"], st_ffn,
        nseq=tl["nseq"], tl=tl["tl_mlp"], cf=cf)
    return (y, cst_mix[:, -1, SUBLANES - (CONV_W - 1):], s_gdn, s_ssd,
            cst_ffn[:, SUBLANES - (FFN_CONV_W - 1):])


def kernel(x_prompt, x_sample, state_conv_mix, state_gdn, state_ssd, state_conv_ffn, c_prompt, c_sample,
           w_ada, b_ada, w_norm_mix, w_in, w_conv_mix, b_conv_mix, gdn_a_log, gdn_dt_bias, gdn_norm,
           ssd_a_log, ssd_dt_bias, ssd_d, ssd_norm, w_branch_gdn, w_branch_ssd, w_out,
           w_norm_ffn, w_ffn_up, w_ffn_conv, b_ffn_conv, w_ffn_down, w_norm_final):
    depth = w_in.shape[0]
    assert depth == 1, "single-layer trunk"
    nb = x_prompt.shape[0]
    ndec = x_sample.shape[0]
    tn = 512
    cf = 256
    l = 0
    p = _prep_params(w_norm_mix[l], w_in[l], w_conv_mix[l], b_conv_mix[l], gdn_a_log[l], gdn_dt_bias[l],
                     gdn_norm[l], ssd_a_log[l], ssd_dt_bias[l], ssd_d[l], ssd_norm[l], w_branch_gdn[l],
                     w_branch_ssd[l], w_out[l], w_norm_ffn[l], w_ffn_up[l], w_ffn_conv[l], b_ffn_conv[l],
                     w_ffn_down[l], w_norm_final, tn)
    c_all = jnp.concatenate([c_prompt, c_sample], axis=0)
    rows = -(-c_all.shape[0] // SUBLANES) * SUBLANES
    mod = _mod_call(jnp.pad(c_all, ((0, rows - c_all.shape[0]), (0, 0))), w_ada[l], b_ada[l])
    dtype = x_prompt.dtype
    zeros = lambda *shape: jnp.zeros(shape, dtype)
    yp, a1, a2, a3, a4 = _trunk(
        x_prompt, mod[:nb],
        zeros(nb, CONV_W - 1, CONV_CH), zeros(nb, GDN_HEADS, GDN_DK, GDN_DV),
        zeros(nb, SSD_HEADS, SSD_STATE, SSD_HEADDIM), zeros(nb, FFN_CONV_W - 1, 2 * D_FF), p, tn, cf)
    ys, b1, b2, b3, b4 = _trunk(
        x_sample, mod[nb:nb + ndec],
        state_conv_mix[l], state_gdn[l], state_ssd[l], state_conv_ffn[l], p, tn, cf)
    return (yp, ys, a1[None], a2[None], a3[None], a4[None], b1[None], b2[None], b3[None], b4[None])
```

```python
import functools

import jax
import jax.numpy as jnp
from jax import lax
from jax.experimental import pallas as pl
from jax.experimental.pallas import tpu as pltpu

F32 = jnp.float32
BF16 = jnp.bfloat16
HIGHEST = lax.Precision.HIGHEST

D_MODEL = 1024
CHUNK = 64
EPS = 1e-6
CONV_W = 4
FFN_CONV_W = 3
GDN_HEADS = 8
GDN_DK = 128
GDN_DV = 128
GDN_QK = GDN_HEADS * GDN_DK
GDN_V = GDN_HEADS * GDN_DV
SSD_INNER = 2 * D_MODEL
SSD_HEADDIM = 64
SSD_HEADS = SSD_INNER // SSD_HEADDIM
SSD_STATE = 128
SSD_GROUPS = 8
SSD_HPG = SSD_HEADS // SSD_GROUPS
SSD_GSZ = SSD_INNER // SSD_GROUPS
SSD_BC = SSD_GROUPS * SSD_STATE
D_FF = 2816
CONV_CH = 2 * GDN_QK + GDN_V + SSD_INNER + 2 * SSD_BC
RAW_CH = GDN_V + SSD_INNER + 2 * D_MODEL
SMALL_CH = 2 * GDN_HEADS + SSD_HEADS
LANES = 128
SUBLANES = 8
LANE_GDN_B = 0
LANE_GDN_A = GDN_HEADS
LANE_SSD_DT = 2 * GDN_HEADS
VMEM_LIMIT_BYTES = 56 * 1024 * 1024


def _sigmoid(x):
    return 1.0 / (1.0 + jnp.exp(-x))


def _silu(x):
    return x * _sigmoid(x)


def _softplus(x):
    return jnp.maximum(x, 0.0) + jnp.log(1.0 + jnp.exp(-jnp.abs(x)))


def _dot(a, b, precision=None):
    return jnp.dot(a, b, preferred_element_type=F32, precision=precision)


def _dot_nt(a, b):
    return lax.dot_general(a, b, (((1,), (1,)), ((), ())), preferred_element_type=F32)


def _dot_tn(a, b):
    return lax.dot_general(a, b, (((0,), (0,)), ((), ())), preferred_element_type=F32)


def _split_bf16(x):
    hi = x.astype(BF16)
    lo = (x - hi.astype(F32)).astype(BF16)
    return hi, lo


def _dot_x3(a, b):
    a_hi, a_lo = a
    b_hi, b_lo = b
    return _dot(a_hi, b_hi) + (_dot(a_hi, b_lo) + _dot(a_lo, b_hi))


def _const_spec(shape):
    zeros = (0,) * len(shape)
    return pl.BlockSpec(shape, lambda *_: zeros)


def _mod_kernel(c_ref, w_ref, b_ref, o_ref):
    o_ref[...] = _dot(_silu(c_ref[...]), w_ref[...], HIGHEST) + b_ref[...]


def _mod_call(c_pad, w_ada, b_ada):
    rows = c_pad.shape[0]
    n = w_ada.shape[1]
    tn = 512
    return pl.pallas_call(
        _mod_kernel,
        grid=(n // tn,),
        in_specs=[pl.BlockSpec((rows, D_MODEL), lambda j: (0, 0)),
                  pl.BlockSpec((D_MODEL, tn), lambda j: (0, j)),
                  pl.BlockSpec((1, tn), lambda j: (0, j))],
        out_specs=pl.BlockSpec((rows, tn), lambda j: (0, j)),
        out_shape=jax.ShapeDtypeStruct((rows, n), F32),
        name="adaln_mod",
    )(c_pad, w_ada, b_ada.reshape(1, n))


def _inproj_kernel(x_ref, mod_ref, wn_ref, w_ref, cw_ref, cb_ref, st_ref,
                   xc_ref, raw_ref, sm_ref, cst_ref,
                   h_scr, tail_scr, stage_scr, *, nseq, tl, n_conv, n_raw):
    m = pl.program_id(1)
    n = pl.program_id(2)

    @pl.when(n == 0)
    def _():
        for s in range(nseq):
            x = x_ref[s]
            ms = jnp.mean(x * x, axis=-1, keepdims=True)
            md = mod_ref[s]
            h = (x * lax.rsqrt(ms + EPS) * wn_ref[...]) * (1.0 + md[1:2]) + md[0:1]
            h_scr[s * tl:(s + 1) * tl, :] = h.astype(BF16)

    def subs(sr):
        return [(s, j * sr) for s in range(nseq) for j in range(tl // sr)]

    def proj(s, lo, sr):
        return _dot(h_scr[s * tl + lo:s * tl + lo + sr, :], w_ref[...])

    sr_conv = min(tl, 128)
    sr_raw = min(tl, 256)

    @pl.when(n < n_conv)
    def _():
        for s in range(nseq):
            @pl.when(m == 0)
            def _():
                tail_scr[n, s] = st_ref[s]
        sr = sr_conv
        for s, lo in subs(sr):
            if lo == 0:
                stage_scr[s, 0:SUBLANES] = tail_scr[n, s]
            stage_scr[s, SUBLANES + lo:SUBLANES + lo + sr] = proj(s, lo, sr)
            y = cb_ref[...] + cw_ref[3:4] * stage_scr[s, 8 + lo:8 + lo + sr]
            y = y + cw_ref[2:3] * stage_scr[s, 7 + lo:7 + lo + sr]
            y = y + cw_ref[1:2] * stage_scr[s, 6 + lo:6 + lo + sr]
            y = y + cw_ref[0:1] * stage_scr[s, 5 + lo:5 + lo + sr]
            xc_ref[s, lo:lo + sr] = _silu(y)
        for s in range(nseq):
            new_tail = stage_scr[s, tl:tl + SUBLANES]
            tail_scr[n, s] = new_tail
            cst_ref[s, 0] = new_tail

    @pl.when(jnp.logical_and(n >= n_conv, n < n_conv + n_raw))
    def _():
        for s, lo in subs(sr_raw):
            raw_ref[s, lo:lo + sr_raw] = proj(s, lo, sr_raw)

    @pl.when(n == n_conv + n_raw)
    def _():
        for s, lo in subs(sr_raw):
            sm_ref[s, lo:lo + sr_raw] = proj(s, lo, sr_raw)[:, 0:LANES]


def _inproj_call(x, mod8, w_norm, w_all, cw8, cb, st8, *, nseq, tl, tn):
    b, t, _ = x.shape
    n_conv = CONV_CH // tn
    n_raw = RAW_CH // tn
    nt = n_conv + n_raw + 1
    assert w_all.shape == (D_MODEL, nt * tn)
    grid = (b // nseq, t // tl, nt)
    conv_idx = lambda n: jnp.minimum(n, n_conv - 1)
    raw_idx = lambda n: jnp.clip(n - n_conv, 0, n_raw - 1)
    kern = functools.partial(_inproj_kernel, nseq=nseq, tl=tl, n_conv=n_conv, n_raw=n_raw)
    return pl.pallas_call(
        kern,
        grid=grid,
        in_specs=[
            pl.BlockSpec((nseq, tl, D_MODEL), lambda i, m, n: (i, m, 0)),
            pl.BlockSpec((nseq, SUBLANES, D_MODEL), lambda i, m, n: (i, 0, 0)),
            pl.BlockSpec((1, D_MODEL), lambda i, m, n: (0, 0)),
            pl.BlockSpec((D_MODEL, tn), lambda i, m, n: (0, n)),
            pl.BlockSpec((SUBLANES, tn), lambda i, m, n: (0, conv_idx(n))),
            pl.BlockSpec((1, tn), lambda i, m, n: (0, conv_idx(n))),
            pl.BlockSpec((nseq, SUBLANES, tn), lambda i, m, n: (i, 0, conv_idx(n))),
        ],
        out_specs=[
            pl.BlockSpec((nseq, tl, tn), lambda i, m, n: (i, m, conv_idx(n))),
            pl.BlockSpec((nseq, tl, tn), lambda i, m, n: (i, m, raw_idx(n))),
            pl.BlockSpec((nseq, tl, LANES), lambda i, m, n: (i, m, 0)),
            pl.BlockSpec((nseq, 1, SUBLANES, tn), lambda i, m, n: (i, m, 0, conv_idx(n))),
        ],
        out_shape=[
            jax.ShapeDtypeStruct((b, t, CONV_CH), F32),
            jax.ShapeDtypeStruct((b, t, RAW_CH), F32),
            jax.ShapeDtypeStruct((b, t, LANES), F32),
            jax.ShapeDtypeStruct((b, t // tl, SUBLANES, CONV_CH), F32),
        ],
        scratch_shapes=[
            pltpu.VMEM((nseq * tl, D_MODEL), BF16),
            pltpu.VMEM((n_conv, nseq, SUBLANES, tn), F32),
            pltpu.VMEM((nseq, tl + SUBLANES, tn), F32),
        ],
        compiler_params=pltpu.CompilerParams(
            dimension_semantics=("arbitrary", "arbitrary", "arbitrary"),
            vmem_limit_bytes=VMEM_LIMIT_BYTES),
        name="inproj_conv",
    )(x, mod8, w_norm, w_all, cw8, cb, st8)


def _chunk_masks():
    row = lax.broadcasted_iota(jnp.int32, (CHUNK, CHUNK), 0)
    col = lax.broadcasted_iota(jnp.int32, (CHUNK, CHUNK), 1)
    return row >= col, row > col, row == col


INV_BLOCK = 16


def _blockdiag_mask():
    row = lax.broadcasted_iota(jnp.int32, (CHUNK, CHUNK), 0)
    col = lax.broadcasted_iota(jnp.int32, (CHUNK, CHUNK), 1)
    return (row // INV_BLOCK) == (col // INV_BLOCK)


def _gdn_kernel(q_ref, k_ref, v_ref, z_ref, sm_ref, s0_ref, aneg_ref, dtb_ref, gn_ref,
                o_ref, sout_ref, s_scr, *, nchunk):
    t = pl.program_id(1)

    @pl.when(t == 0)
    def _():
        s_scr[...] = s0_ref[0]

    causal, strict, diag = _chunk_masks()
    tril_b = causal.astype(BF16)
    eye = diag.astype(F32)
    blockdiag = _blockdiag_mask()
    qscale = GDN_DK ** -0.5

    heads = range(GDN_HEADS)
    hcols = [slice(h * GDN_DK, (h + 1) * GDN_DK) for h in heads]

    def chunk(c, carry):
        r0 = pl.multiple_of(c * CHUNK, CHUNK)
        rows = pl.ds(r0, CHUNK)
        sm = sm_ref[0, rows, :]
        beta_all = _sigmoid(sm)
        g_all = aneg_ref[...] * _softplus(sm + dtb_ref[...])
        g_hi, g_lo = _split_bf16(g_all)
        gam_all = _dot(tril_b, g_hi) + _dot(tril_b, g_lo)
        gam_t = gam_all.T
        egam_all = jnp.exp(gam_all)
        last_all = gam_all[CHUNK - 1:CHUNK, :]
        elast_all = jnp.exp(last_all)
        erem_all = jnp.exp(last_all - gam_all)
        b_i = [beta_all[:, LANE_GDN_B + h:LANE_GDN_B + h + 1] for h in heads]
        egam_i = [egam_all[:, LANE_GDN_A + h:LANE_GDN_A + h + 1] for h in heads]
        dec = [jnp.exp(jnp.where(causal, gam_all[:, LANE_GDN_A + h:LANE_GDN_A + h + 1]
                                 - gam_t[LANE_GDN_A + h:LANE_GDN_A + h + 1, :], -jnp.inf)) for h in heads]
        kh, kq = [], []
        for h in heads:
            qr = q_ref[0, rows, hcols[h]]
            kr = k_ref[0, rows, hcols[h]]
            qn = qr * (lax.rsqrt(jnp.sum(qr * qr, axis=-1, keepdims=True) + EPS) * qscale)
            kn = kr * lax.rsqrt(jnp.sum(kr * kr, axis=-1, keepdims=True) + EPS)
            kh.append(kn)
            kq.append(jnp.concatenate([kn.astype(BF16), qn.astype(BF16)], axis=0))
        kqk = [_dot_nt(kq[h], kq[h][0:CHUNK]) for h in heads]
        x = [jnp.where(strict, kqk[h][0:CHUNK] * dec[h] * (-b_i[h]), 0.0) for h in heads]
        xd = [jnp.where(blockdiag, x[h], 0.0) for h in heads]
        xn = [_split_bf16(jnp.where(blockdiag, 0.0, x[h])) for h in heads]
        z = [_split_bf16(xd[h]) for h in heads]
        p = [eye + xd[h] for h in heads]
        for _ in range(3):
            zf = [_dot_x3(z[h], z[h]) for h in heads]
            z = [_split_bf16(zf[h]) for h in heads]
            p = [p[h] + _dot_x3(_split_bf16(p[h]), z[h]) for h in heads]
        t0 = [_split_bf16(p[h]) for h in heads]
        m = [_dot_x3(t0[h], xn[h]) for h in heads]
        ms = [_split_bf16(m[h]) for h in heads]
        m2 = [_split_bf16(_dot_x3(ms[h], ms[h])) for h in heads]
        ipm = [eye + m[h] for h in heads]
        qm = [ipm[h] + _dot_x3(_split_bf16(ipm[h]), m2[h]) for h in heads]
        tinv = [_split_bf16(_dot_x3(_split_bf16(qm[h]), t0[h])) for h in heads]
        rhs = [_split_bf16(jnp.concatenate([v_ref[0, rows, hcols[h]] * b_i[h],
                                            kh[h] * (b_i[h] * egam_i[h])], axis=1)) for h in heads]
        sol = [_dot_x3(tinv[h], rhs[h]) for h in heads]
        s = [s_scr[h] for h in heads]
        sb = [s[h].astype(BF16) for h in heads]
        wq = [jnp.concatenate([sol[h][:, GDN_DV:].astype(BF16), kq[h][CHUNK:]], axis=0) for h in heads]
        ws = [_dot(wq[h], sb[h]) for h in heads]
        ub = [(sol[h][:, :GDN_DV] - ws[h][0:CHUNK]).astype(BF16) for h in heads]
        qk = [(kqk[h][CHUNK:] * dec[h]).astype(BF16) for h in heads]
        o = [egam_i[h] * ws[h][CHUNK:] + _dot(qk[h], ub[h]) for h in heads]
        for h in heads:
            lg = LANE_GDN_A + h
            kdec = (kh[h] * erem_all[:, lg:lg + 1]).astype(BF16)
            s_scr[h] = elast_all[:, lg:lg + 1] * s[h] + _dot_tn(kdec, ub[h])
        for h in heads:
            zh = z_ref[0, rows, hcols[h]]
            on = (o[h] * lax.rsqrt(jnp.mean(o[h] * o[h], axis=-1, keepdims=True) + EPS)
                  * gn_ref[...] * _silu(zh))
            o_ref[0, rows, hcols[h]] = on.astype(o_ref.dtype)
        return carry

    lax.fori_loop(0, nchunk, chunk, 0)

    @pl.when(t == pl.num_programs(1) - 1)
    def _():
        sout_ref[0] = s_scr[...]


def _gdn_call(xc, raw, small, s0, aneg_l, dtb_l, gnorm, *, tc):
    b, t, _ = xc.shape
    kern = functools.partial(_gdn_kernel, nchunk=tc // CHUNK)
    wide = lambda j: pl.BlockSpec((1, tc, GDN_QK), lambda i, m: (i, m, j))
    state = pl.BlockSpec((1, GDN_HEADS, GDN_DK, GDN_DV), lambda i, m: (i, 0, 0, 0))
    return pl.pallas_call(
        kern,
        grid=(b, t // tc),
        in_specs=[wide(0), wide(1), wide(2), wide(0),
                  pl.BlockSpec((1, tc, LANES), lambda i, m: (i, m, 0)),
                  state,
                  _const_spec((1, LANES)), _const_spec((1, LANES)), _const_spec((1, GDN_DV))],
        out_specs=[pl.BlockSpec((1, tc, GDN_V), lambda i, m: (i, m, 0)), state],
        out_shape=[jax.ShapeDtypeStruct((b, t, GDN_V), BF16),
                   jax.ShapeDtypeStruct((b, GDN_HEADS, GDN_DK, GDN_DV), F32)],
        scratch_shapes=[pltpu.VMEM((GDN_HEADS, GDN_DK, GDN_DV), F32)],
        compiler_params=pltpu.CompilerParams(
            dimension_semantics=("arbitrary", "arbitrary"),
            vmem_limit_bytes=VMEM_LIMIT_BYTES),
        name="gdn_scan",
    )(xc, xc, xc, raw, small, s0, aneg_l, dtb_l, gnorm)


def _ssd_kernel(xlo_ref, xhi_ref, bm_ref, cm_ref, zlo_ref, zhi_ref, sm_ref, h0_ref,
                aneg_ref, dtb_ref, dexp_ref, nrm_ref,
                y_ref, hout_ref, h_scr, *, nchunk):
    t = pl.program_id(1)

    @pl.when(t == 0)
    def _():
        h_scr[...] = h0_ref[0]

    pair_w = 2 * SSD_HEADDIM
    row = lax.broadcasted_iota(jnp.int32, (CHUNK, pair_w), 0)
    lane = lax.broadcasted_iota(jnp.int32, (CHUNK, pair_w), 1)
    causal2 = row >= (lane % CHUNK)
    first = lane < SSD_HEADDIM
    rr = lax.broadcasted_iota(jnp.int32, (pair_w, pair_w), 0)
    cc = lax.broadcasted_iota(jnp.int32, (pair_w, pair_w), 1)
    pair_diag = (rr // SSD_HEADDIM) == (cc // SSD_HEADDIM)
    row_c = lax.broadcasted_iota(jnp.int32, (CHUNK, CHUNK), 0)
    col_c = lax.broadcasted_iota(jnp.int32, (CHUNK, CHUNK), 1)
    tril_b = (row_c >= col_c).astype(BF16)
    half = SSD_GROUPS // 2
    groups = range(SSD_GROUPS)
    pairs = range(SSD_HEADS // 2)
    ppg = SSD_HPG // 2

    def chunk(c, carry):
        r0 = pl.multiple_of(c * CHUNK, CHUNK)
        rows = pl.ds(r0, CHUNK)
        sm = sm_ref[0, rows, :]
        dt_all = _softplus(sm + dtb_ref[...])
        da_hi, da_lo = _split_bf16(dt_all * aneg_ref[...])
        gam_all = _dot(tril_b, da_hi) + _dot(tril_b, da_lo)
        gam_t2 = jnp.concatenate([gam_all, pltpu.roll(gam_all, LANES - 1, 1)], axis=0).T
        dt_t2 = jnp.concatenate([dt_all, pltpu.roll(dt_all, LANES - 1, 1)], axis=0).T
        gam_i2, dt_i2 = [], []
        for p in pairs:
            l = LANE_SSD_DT + 2 * p
            gam_i2.append(jnp.where(first, gam_all[:, l:l + 1], gam_all[:, l + 1:l + 2]))
            dt_i2.append(jnp.where(first, dt_all[:, l:l + 1], dt_all[:, l + 1:l + 2]))
        dec2 = [jnp.exp(jnp.where(causal2, gam_i2[p] - gam_t2[LANE_SSD_DT + 2 * p:LANE_SSD_DT + 2 * p + 1, :],
                                  -jnp.inf)) for p in pairs]
        egam2 = [jnp.exp(gam_i2[p]) for p in pairs]
        last2 = [gam_i2[p][CHUNK - 1:CHUNK, :] for p in pairs]
        wj2 = [jnp.exp(last2[p] - gam_i2[p]) * dt_i2[p] for p in pairs]
        elast2 = [jnp.exp(last2[p]) for p in pairs]

        xg, bmb, cmb = [], [], []
        for g in groups:
            x_ref = xlo_ref if g < half else xhi_ref
            xg.append(x_ref[0, rows, (g % half) * SSD_GSZ:(g % half + 1) * SSD_GSZ])
            bmb.append(bm_ref[0, rows, g * SSD_STATE:(g + 1) * SSD_STATE].astype(BF16))
            cmb.append(cm_ref[0, rows, g * SSD_STATE:(g + 1) * SSD_STATE].astype(BF16))
        cb2 = [_dot_nt(cmb[g], jnp.concatenate([bmb[g], bmb[g]], axis=0)) for g in groups]
        ypair = []
        for p in pairs:
            g = p // ppg
            xp = xg[g][:, (p % ppg) * pair_w:(p % ppg + 1) * pair_w].astype(BF16)
            xbd = jnp.where(pair_diag, jnp.concatenate([xp, xp], axis=0), 0.0)
            w2 = cb2[g] * dec2[p] * dt_t2[LANE_SSD_DT + 2 * p:LANE_SSD_DT + 2 * p + 1, :]
            ypair.append(_dot(w2.astype(BF16), xbd))
        hg = [h_scr[g] for g in groups]
        ch = [_dot(cmb[g], hg[g].astype(BF16)) for g in groups]
        for g in groups:
            ps = range(g * ppg, (g + 1) * ppg)
            wj = jnp.concatenate([wj2[p] for p in ps], axis=1)
            elast = jnp.concatenate([elast2[p] for p in ps], axis=1)
            h_scr[g] = elast * hg[g] + _dot_tn(bmb[g], (xg[g] * wj).astype(BF16))
        for g in groups:
            ps = range(g * ppg, (g + 1) * ppg)
            z_ref = zlo_ref if g < half else zhi_ref
            ocols = slice(g * SSD_GSZ, (g + 1) * SSD_GSZ)
            egam = jnp.concatenate([egam2[p] for p in ps], axis=1)
            y = jnp.concatenate([ypair[p] for p in ps], axis=1) + ch[g] * egam
            y = y + dexp_ref[:, ocols] * xg[g]
            yz = y * _silu(z_ref[0, rows, (g % half) * SSD_GSZ:(g % half + 1) * SSD_GSZ])
            yn = yz * lax.rsqrt(jnp.mean(yz * yz, axis=-1, keepdims=True) + EPS) * nrm_ref[:, ocols]
            y_ref[0, rows, ocols] = yn.astype(y_ref.dtype)
        return carry

    lax.fori_loop(0, nchunk, chunk, 0)

    @pl.when(t == pl.num_programs(1) - 1)
    def _():
        hout_ref[0] = h_scr[...]


def _ssd_call(xc, raw, small, h0g, aneg_l, dtb_l, dexp, nrm, *, tc):
    b, t, _ = xc.shape
    kern = functools.partial(_ssd_kernel, nchunk=tc // CHUNK)
    wide = lambda j: pl.BlockSpec((1, tc, 1024), lambda i, m: (i, m, j))
    state = pl.BlockSpec((1, SSD_GROUPS, SSD_STATE, SSD_GSZ), lambda i, m: (i, 0, 0, 0))
    return pl.pallas_call(
        kern,
        grid=(b, t // tc),
        in_specs=[wide(3), wide(4), wide(5), wide(6), wide(1), wide(2),
                  pl.BlockSpec((1, tc, LANES), lambda i, m: (i, m, 0)),
                  state,
                  _const_spec((1, LANES)), _const_spec((1, LANES)),
                  _const_spec((1, SSD_INNER)), _const_spec((1, SSD_INNER))],
        out_specs=[pl.BlockSpec((1, tc, SSD_INNER), lambda i, m: (i, m, 0)), state],
        out_shape=[jax.ShapeDtypeStruct((b, t, SSD_INNER), BF16),
                   jax.ShapeDtypeStruct((b, SSD_GROUPS, SSD_STATE, SSD_GSZ), F32)],
        scratch_shapes=[pltpu.VMEM((SSD_GROUPS, SSD_STATE, SSD_GSZ), F32)],
        compiler_params=pltpu.CompilerParams(
            dimension_semantics=("arbitrary", "arbitrary"),
            vmem_limit_bytes=VMEM_LIMIT_BYTES),
        name="ssd_scan",
    )(xc, xc, xc, xc, raw, raw, small, h0g, aneg_l, dtb_l, dexp, nrm)


def _mlp_kernel(x_ref, og_ref, ys_ref, ga_ref, gb_ref, mod_ref,
                wbg_ref, wbs_ref, wo_ref, wn2_ref, wup_ref, cw_ref, cb_ref, wdn_ref, wnf_ref, st_ref,
                out_ref, cst_ref, tail_scr, sa_scr, sv_scr, *, nseq, tl, cf):
    m = pl.program_id(1)
    for s in range(nseq):
        @pl.when(m == 0)
        def _():
            tail_scr[s] = st_ref[s]

        md = mod_ref[s]
        mixed = (_sigmoid(ga_ref[s]) * _dot(og_ref[s], wbg_ref[...])
                 + _sigmoid(gb_ref[s]) * _dot(ys_ref[s], wbs_ref[...]))
        x1 = x_ref[s] + md[2:3] * _dot(mixed.astype(BF16), wo_ref[...])
        ms = jnp.mean(x1 * x1, axis=-1, keepdims=True)
        h2 = ((x1 * lax.rsqrt(ms + EPS) * wn2_ref[...]) * (1.0 + md[4:5]) + md[3:4]).astype(BF16)

        def conv(stage, u, cols):
            stage[0:SUBLANES] = tail_scr[s, :, cols]
            stage[SUBLANES:SUBLANES + tl] = u
            y = cb_ref[:, cols] + cw_ref[2:3, cols] * stage[8:8 + tl]
            y = y + cw_ref[1:2, cols] * stage[7:7 + tl]
            y = y + cw_ref[0:1, cols] * stage[6:6 + tl]
            tail_scr[s, :, cols] = stage[tl:tl + SUBLANES]
            return y

        def up(j):
            ca = slice(j * cf, (j + 1) * cf)
            cv = slice(D_FF + j * cf, D_FF + (j + 1) * cf)
            return ca, cv, _dot(h2, wup_ref[:, ca]), _dot(h2, wup_ref[:, cv])

        acc = jnp.zeros((tl, D_MODEL), F32)
        nxt = up(0)
        for j in range(D_FF // cf):
            ca, cv, pa, pv = nxt
            if j + 1 < D_FF // cf:
                nxt = up(j + 1)
            ua = conv(sa_scr.at[j % 2], pa, ca)
            uv = conv(sv_scr.at[j % 2], pv, cv)
            act = (_silu(ua) * uv).astype(BF16)
            acc = acc + _dot(act, wdn_ref[ca, :])
        x2 = x1 + md[5:6] * acc
        ms2 = jnp.mean(x2 * x2, axis=-1, keepdims=True)
        out_ref[s] = x2 * lax.rsqrt(ms2 + EPS) * wnf_ref[...]
        cst_ref[s] = tail_scr[s]


def _mlp_call(x, og, ys, raw, mod8, wbg, wbs, wo, wn2, wup, cw8, cb, wdn, wnf, st8, *, nseq, tl, cf):
    b, t, _ = x.shape
    kern = functools.partial(_mlp_kernel, nseq=nseq, tl=tl, cf=cf)
    tok = lambda width, j: pl.BlockSpec((nseq, tl, width), lambda i, m: (i, m, j))
    per_seq = lambda width: pl.BlockSpec((nseq, SUBLANES, width), lambda i, m: (i, 0, 0))

    def resident(shape):
        zeros = (0,) * len(shape)
        return pl.BlockSpec(shape, lambda i, m: zeros, pipeline_mode=pl.Buffered(1))

    return pl.pallas_call(
        kern,
        grid=(b // nseq, t // tl),
        in_specs=[tok(D_MODEL, 0), tok(GDN_V, 0), tok(SSD_INNER, 0), tok(D_MODEL, 3), tok(D_MODEL, 4),
                  per_seq(D_MODEL),
                  resident((GDN_V, D_MODEL)), resident((SSD_INNER, D_MODEL)), resident((D_MODEL, D_MODEL)),
                  resident((1, D_MODEL)), resident((D_MODEL, 2 * D_FF)),
                  resident((SUBLANES, 2 * D_FF)), resident((1, 2 * D_FF)),
                  resident((D_FF, D_MODEL)), resident((1, D_MODEL)),
                  per_seq(2 * D_FF)],
        out_specs=[tok(D_MODEL, 0), per_seq(2 * D_FF)],
        out_shape=[jax.ShapeDtypeStruct((b, t, D_MODEL), F32),
                   jax.ShapeDtypeStruct((b, SUBLANES, 2 * D_FF), F32)],
        scratch_shapes=[pltpu.VMEM((nseq, SUBLANES, 2 * D_FF), F32),
                        pltpu.VMEM((2, tl + SUBLANES, cf), F32),
                        pltpu.VMEM((2, tl + SUBLANES, cf), F32)],
        compiler_params=pltpu.CompilerParams(
            dimension_semantics=("arbitrary", "arbitrary"),
            vmem_limit_bytes=VMEM_LIMIT_BYTES),
        name="merge_ffn",
    )(x, og, ys, raw, raw, mod8, wbg, wbs, wo, wn2, wup, cw8, cb, wdn, wnf, st8)


def _lane_vec(values, offset):
    return jnp.zeros((1, LANES), F32).at[0, offset:offset + values.shape[0]].set(values.astype(F32))


def _prep_params(w_norm_mix, w_in, w_conv_mix, b_conv_mix, gdn_a_log, gdn_dt_bias, gdn_norm,
                 ssd_a_log, ssd_dt_bias, ssd_d, ssd_norm, w_branch_gdn, w_branch_ssd, w_out,
                 w_norm_ffn, w_ffn_up, w_ffn_conv, b_ffn_conv, w_ffn_down, w_norm_final, tn):
    o_z = CONV_CH
    o_small = CONV_CH + GDN_V + SSD_INNER
    o_merge = o_small + SMALL_CH
    w_small = jnp.pad(w_in[:, o_small:o_merge], ((0, 0), (0, tn - SMALL_CH)))
    w_all = jnp.concatenate([w_in[:, :o_small], w_in[:, o_merge:], w_small], axis=1).astype(BF16)
    del o_z
    return dict(
        w_norm_mix=w_norm_mix.reshape(1, D_MODEL),
        w_all=w_all,
        cw_mix=jnp.pad(w_conv_mix, ((0, SUBLANES - CONV_W), (0, 0))),
        cb_mix=b_conv_mix.reshape(1, CONV_CH),
        gdn_aneg=_lane_vec(-jnp.exp(gdn_a_log.astype(F32)), LANE_GDN_A),
        gdn_dtb=_lane_vec(gdn_dt_bias, LANE_GDN_A),
        gdn_norm=gdn_norm.reshape(1, GDN_DV).astype(F32),
        ssd_aneg=_lane_vec(-jnp.exp(ssd_a_log.astype(F32)), LANE_SSD_DT),
        ssd_dtb=_lane_vec(ssd_dt_bias, LANE_SSD_DT),
        ssd_dexp=jnp.repeat(ssd_d.astype(F32), SSD_HEADDIM).reshape(1, SSD_INNER),
        ssd_norm=ssd_norm.reshape(1, SSD_INNER).astype(F32),
        wbg=w_branch_gdn.astype(BF16), wbs=w_branch_ssd.astype(BF16), wo=w_out.astype(BF16),
        w_norm_ffn=w_norm_ffn.reshape(1, D_MODEL),
        wup=w_ffn_up.astype(BF16),
        cw_ffn=jnp.pad(w_ffn_conv, ((0, SUBLANES - FFN_CONV_W), (0, 0))),
        cb_ffn=b_ffn_conv.reshape(1, 2 * D_FF),
        wdn=w_ffn_down.astype(BF16),
        w_norm_final=w_norm_final.reshape(1, D_MODEL),
    )


def _tiles(b, t):
    if t <= CHUNK:
        return dict(nseq=b, tl_in=t, tl_mlp=t, tc=t)
    tl_in = min(t, 1024)
    tl_mlp = min(t, 256)
    tc = min(t, 256)
    return dict(nseq=1, tl_in=tl_in, tl_mlp=tl_mlp, tc=tc)


def _trunk(x, mod, conv_mix0, s_gdn0, s_ssd0, conv_ffn0, p, tn, cf):
    b, t, _ = x.shape
    tl = _tiles(b, t)
    mod8 = jnp.pad(mod.reshape(b, 6, D_MODEL), ((0, 0), (0, SUBLANES - 6), (0, 0)))
    st_mix = jnp.pad(conv_mix0, ((0, 0), (SUBLANES - (CONV_W - 1), 0), (0, 0)))
    xc, raw, small, cst_mix = _inproj_call(
        x, mod8, p["w_norm_mix"], p["w_all"], p["cw_mix"], p["cb_mix"], st_mix,
        nseq=tl["nseq"], tl=tl["tl_in"], tn=tn)
    og, s_gdn = _gdn_call(xc, raw, small, s_gdn0, p["gdn_aneg"], p["gdn_dtb"], p["gdn_norm"], tc=tl["tc"])
    h0g = s_ssd0.reshape(b, SSD_GROUPS, SSD_HPG, SSD_STATE, SSD_HEADDIM)
    h0g = jnp.swapaxes(h0g, 2, 3).reshape(b, SSD_GROUPS, SSD_STATE, SSD_GSZ)
    ys, hg = _ssd_call(xc, raw, small, h0g, p["ssd_aneg"], p["ssd_dtb"], p["ssd_dexp"], p["ssd_norm"],
                       tc=tl["tc"])
    s_ssd = jnp.swapaxes(hg.reshape(b, SSD_GROUPS, SSD_STATE, SSD_HPG, SSD_HEADDIM), 2, 3)
    s_ssd = s_ssd.reshape(b, SSD_HEADS, SSD_STATE, SSD_HEADDIM)
    st_ffn = jnp.pad(conv_ffn0, ((0, 0), (SUBLANES - (FFN_CONV_W - 1), 0), (0, 0)))
    y, cst_ffn = _mlp_call(
        x, og, ys, raw, mod8, p["wbg"], p["wbs"], p["wo"], p["w_norm_ffn"], p["wup"],
        p["cw_ffn"], p["cb_ffn"], p["wdn"], p["w_norm_final"], st_ffn,
        nseq=tl["nseq"], tl=tl["tl_mlp"], cf=cf)
    return (y, cst_mix[:, -1, SUBLANES - (CONV_W - 1):], s_gdn, s_ssd,
            cst_ffn[:, SUBLANES - (FFN_CONV_W - 1):])


def kernel(x_prompt, x_sample, state_conv_mix, state_gdn, state_ssd, state_conv_ffn, c_prompt, c_sample,
           w_ada, b_ada, w_norm_mix, w_in, w_conv_mix, b_conv_mix, gdn_a_log, gdn_dt_bias, gdn_norm,
           ssd_a_log, ssd_dt_bias, ssd_d, ssd_norm, w_branch_gdn, w_branch_ssd, w_out,
           w_norm_ffn, w_ffn_up, w_ffn_conv, b_ffn_conv, w_ffn_down, w_norm_final):
    depth = w_in.shape[0]
    assert depth == 1, "single-layer trunk"
    nb = x_prompt.shape[0]
    ndec = x_sample.shape[0]
    tn = 512
    cf = 256
    l = 0
    p = _prep_params(w_norm_mix[l], w_in[l], w_conv_mix[l], b_conv_mix[l], gdn_a_log[l], gdn_dt_bias[l],
                     gdn_norm[l], ssd_a_log[l], ssd_dt_bias[l], ssd_d[l], ssd_norm[l], w_branch_gdn[l],
                     w_branch_ssd[l], w_out[l], w_norm_ffn[l], w_ffn_up[l], w_ffn_conv[l], b_ffn_conv[l],
                     w_ffn_down[l], w_norm_final, tn)
    c_all = jnp.concatenate([c_prompt, c_sample], axis=0)
    rows = -(-c_all.shape[0] // SUBLANES) * SUBLANES
    mod = _mod_call(jnp.pad(c_all, ((0, rows - c_all.shape[0]), (0, 0))), w_ada[l], b_ada[l])
    dtype = x_prompt.dtype
    zeros = lambda *shape: jnp.zeros(shape, dtype)
    yp, a1, a2, a3, a4 = _trunk(
        x_prompt, mod[:nb],
        zeros(nb, CONV_W - 1, CONV_CH), zeros(nb, GDN_HEADS, GDN_DK, GDN_DV),
        zeros(nb, SSD_HEADS, SSD_STATE, SSD_HEADDIM), zeros(nb, FFN_CONV_W - 1, 2 * D_FF), p, tn, cf)
    ys, b1, b2, b3, b4 = _trunk(
        x_sample, mod[nb:nb + ndec],
        state_conv_mix[l], state_gdn[l], state_ssd[l], state_conv_ffn[l], p, tn, cf)
    return (yp, ys, a1[None], a2[None], a3[None], a4[None], b1[None], b2[None], b3[None], b4[None])
```

```python
import functools

import jax
import jax.numpy as jnp
from jax import lax
from jax.experimental import pallas as pl
from jax.experimental.pallas import tpu as pltpu

F32 = jnp.float32
BF16 = jnp.bfloat16
HIGHEST = lax.Precision.HIGHEST

D_MODEL = 1024
CHUNK = 64
EPS = 1e-6
CONV_W = 4
FFN_CONV_W = 3
GDN_HEADS = 8
GDN_DK = 128
GDN_DV = 128
GDN_QK = GDN_HEADS * GDN_DK
GDN_V = GDN_HEADS * GDN_DV
SSD_INNER = 2 * D_MODEL
SSD_HEADDIM = 64
SSD_HEADS = SSD_INNER // SSD_HEADDIM
SSD_STATE = 128
SSD_GROUPS = 8
SSD_HPG = SSD_HEADS // SSD_GROUPS
SSD_GSZ = SSD_INNER // SSD_GROUPS
SSD_BC = SSD_GROUPS * SSD_STATE
D_FF = 2816
CONV_CH = 2 * GDN_QK + GDN_V + SSD_INNER + 2 * SSD_BC
RAW_CH = GDN_V + SSD_INNER + 2 * D_MODEL
SMALL_CH = 2 * GDN_HEADS + SSD_HEADS
LANES = 128
SUBLANES = 8
LANE_GDN_B = 0
LANE_GDN_A = GDN_HEADS
LANE_SSD_DT = 2 * GDN_HEADS
VMEM_LIMIT_BYTES = 56 * 1024 * 1024


def _sigmoid(x):
    return 1.0 / (1.0 + jnp.exp(-x))


def _silu(x):
    return x * _sigmoid(x)


def _softplus(x):
    return jnp.maximum(x, 0.0) + jnp.log(1.0 + jnp.exp(-jnp.abs(x)))


def _dot(a, b, precision=None):
    return jnp.dot(a, b, preferred_element_type=F32, precision=precision)


def _dot_nt(a, b):
    return lax.dot_general(a, b, (((1,), (1,)), ((), ())), preferred_element_type=F32)


def _dot_tn(a, b):
    return lax.dot_general(a, b, (((0,), (0,)), ((), ())), preferred_element_type=F32)


def _split_bf16(x):
    hi = x.astype(BF16)
    lo = (x - hi.astype(F32)).astype(BF16)
    return hi, lo


def _dot_x3(a, b):
    a_hi, a_lo = a
    b_hi, b_lo = b
    return _dot(a_hi, b_hi) + (_dot(a_hi, b_lo) + _dot(a_lo, b_hi))


def _const_spec(shape):
    zeros = (0,) * len(shape)
    return pl.BlockSpec(shape, lambda *_: zeros)


def _mod_kernel(c_ref, w_ref, b_ref, o_ref):
    o_ref[...] = _dot(_silu(c_ref[...]), w_ref[...], HIGHEST) + b_ref[...]


def _mod_call(c_pad, w_ada, b_ada):
    rows = c_pad.shape[0]
    n = w_ada.shape[1]
    tn = 512
    return pl.pallas_call(
        _mod_kernel,
        grid=(n // tn,),
        in_specs=[pl.BlockSpec((rows, D_MODEL), lambda j: (0, 0)),
                  pl.BlockSpec((D_MODEL, tn), lambda j: (0, j)),
                  pl.BlockSpec((1, tn), lambda j: (0, j))],
        out_specs=pl.BlockSpec((rows, tn), lambda j: (0, j)),
        out_shape=jax.ShapeDtypeStruct((rows, n), F32),
        name="adaln_mod",
    )(c_pad, w_ada, b_ada.reshape(1, n))


def _inproj_kernel(x_ref, mod_ref, wn_ref, w_ref, ws_ref, cw_ref, cb_ref, st_ref,
                   xc_ref, raw_ref, sm_ref, cst_ref,
                   h_scr, tail_scr, stage_scr, *, nseq, tl, n_conv):
    m = pl.program_id(1)
    n = pl.program_id(2)

    @pl.when(n == 0)
    def _():
        for s in range(nseq):
            x = x_ref[s]
            ms = jnp.mean(x * x, axis=-1, keepdims=True)
            md = mod_ref[s]
            h = (x * lax.rsqrt(ms + EPS) * wn_ref[...]) * (1.0 + md[1:2]) + md[0:1]
            hb = h.astype(BF16)
            h_scr[s * tl:(s + 1) * tl, :] = hb
            sm_ref[s] = _dot(hb, ws_ref[...])

    def subs(sr):
        return [(s, j * sr) for s in range(nseq) for j in range(tl // sr)]

    def proj(s, lo, sr):
        return _dot(h_scr[s * tl + lo:s * tl + lo + sr, :], w_ref[...])

    sr_conv = min(tl, 128)
    sr_raw = min(tl, 256)

    @pl.when(n < n_conv)
    def _():
        for s in range(nseq):
            @pl.when(m == 0)
            def _():
                tail_scr[n, s] = st_ref[s]
        sr = sr_conv
        for s, lo in subs(sr):
            if lo == 0:
                stage_scr[s, 0:SUBLANES] = tail_scr[n, s]
            stage_scr[s, SUBLANES + lo:SUBLANES + lo + sr] = proj(s, lo, sr)
            y = cb_ref[...] + cw_ref[3:4] * stage_scr[s, 8 + lo:8 + lo + sr]
            y = y + cw_ref[2:3] * stage_scr[s, 7 + lo:7 + lo + sr]
            y = y + cw_ref[1:2] * stage_scr[s, 6 + lo:6 + lo + sr]
            y = y + cw_ref[0:1] * stage_scr[s, 5 + lo:5 + lo + sr]
            xc_ref[s, lo:lo + sr] = _silu(y).astype(xc_ref.dtype)
        for s in range(nseq):
            new_tail = stage_scr[s, tl:tl + SUBLANES]
            tail_scr[n, s] = new_tail
            cst_ref[s, 0] = new_tail

    @pl.when(n >= n_conv)
    def _():
        for s, lo in subs(sr_raw):
            raw_ref[s, lo:lo + sr_raw] = proj(s, lo, sr_raw).astype(raw_ref.dtype)


def _inproj_call(x, mod8, w_norm, w_all, w_small, cw8, cb, st8, *, nseq, tl, tn):
    b, t, _ = x.shape
    n_conv = CONV_CH // tn
    n_raw = RAW_CH // tn
    nt = n_conv + n_raw
    assert w_all.shape == (D_MODEL, nt * tn)
    grid = (b // nseq, t // tl, nt)
    conv_idx = lambda n: jnp.minimum(n, n_conv - 1)
    raw_idx = lambda n: jnp.maximum(n - n_conv, 0)
    kern = functools.partial(_inproj_kernel, nseq=nseq, tl=tl, n_conv=n_conv)
    return pl.pallas_call(
        kern,
        grid=grid,
        in_specs=[
            pl.BlockSpec((nseq, tl, D_MODEL), lambda i, m, n: (i, m, 0)),
            pl.BlockSpec((nseq, SUBLANES, D_MODEL), lambda i, m, n: (i, 0, 0)),
            pl.BlockSpec((1, D_MODEL), lambda i, m, n: (0, 0)),
            pl.BlockSpec((D_MODEL, tn), lambda i, m, n: (0, n)),
            pl.BlockSpec((D_MODEL, LANES), lambda i, m, n: (0, 0)),
            pl.BlockSpec((SUBLANES, tn), lambda i, m, n: (0, conv_idx(n))),
            pl.BlockSpec((1, tn), lambda i, m, n: (0, conv_idx(n))),
            pl.BlockSpec((nseq, SUBLANES, tn), lambda i, m, n: (i, 0, conv_idx(n))),
        ],
        out_specs=[
            pl.BlockSpec((nseq, tl, tn), lambda i, m, n: (i, m, conv_idx(n))),
            pl.BlockSpec((nseq, tl, tn), lambda i, m, n: (i, m, raw_idx(n))),
            pl.BlockSpec((nseq, tl, LANES), lambda i, m, n: (i, m, 0)),
            pl.BlockSpec((nseq, 1, SUBLANES, tn), lambda i, m, n: (i, m, 0, conv_idx(n))),
        ],
        out_shape=[
            jax.ShapeDtypeStruct((b, t, CONV_CH), BF16),
            jax.ShapeDtypeStruct((b, t, RAW_CH), BF16),
            jax.ShapeDtypeStruct((b, t, LANES), F32),
            jax.ShapeDtypeStruct((b, t // tl, SUBLANES, CONV_CH), F32),
        ],
        scratch_shapes=[
            pltpu.VMEM((nseq * tl, D_MODEL), BF16),
            pltpu.VMEM((n_conv, nseq, SUBLANES, tn), F32),
            pltpu.VMEM((nseq, tl + SUBLANES, tn), F32),
        ],
        compiler_params=pltpu.CompilerParams(
            dimension_semantics=("arbitrary", "arbitrary", "arbitrary"),
            vmem_limit_bytes=VMEM_LIMIT_BYTES),
        name="inproj_conv",
    )(x, mod8, w_norm, w_all, w_small, cw8, cb, st8)


def _chunk_masks():
    row = lax.broadcasted_iota(jnp.int32, (CHUNK, CHUNK), 0)
    col = lax.broadcasted_iota(jnp.int32, (CHUNK, CHUNK), 1)
    return row >= col, row > col, row == col


INV_BLOCK = 16


def _blockdiag_mask():
    row = lax.broadcasted_iota(jnp.int32, (CHUNK, CHUNK), 0)
    col = lax.broadcasted_iota(jnp.int32, (CHUNK, CHUNK), 1)
    return (row // INV_BLOCK) == (col // INV_BLOCK)


def _gdn_kernel(q_ref, k_ref, v_ref, z_ref, sm_ref, s0_ref, aneg_ref, dtb_ref, gn_ref,
                o_ref, sout_ref, s_scr, *, nchunk):
    t = pl.program_id(1)

    @pl.when(t == 0)
    def _():
        s_scr[...] = s0_ref[0]

    causal, strict, diag = _chunk_masks()
    tril_b = causal.astype(BF16)
    eye = diag.astype(F32)
    blockdiag = _blockdiag_mask()
    qscale = GDN_DK ** -0.5

    heads = range(GDN_HEADS)
    hcols = [slice(h * GDN_DK, (h + 1) * GDN_DK) for h in heads]

    def chunk(c, carry):
        r0 = pl.multiple_of(c * CHUNK, CHUNK)
        rows = pl.ds(r0, CHUNK)
        sm = sm_ref[0, rows, :]
        beta_all = _sigmoid(sm)
        g_all = aneg_ref[...] * _softplus(sm + dtb_ref[...])
        g_hi, g_lo = _split_bf16(g_all)
        gam_all = _dot(tril_b, g_hi) + _dot(tril_b, g_lo)
        gam_t = gam_all.T
        egam_all = jnp.exp(gam_all)
        last_all = gam_all[CHUNK - 1:CHUNK, :]
        elast_all = jnp.exp(last_all)
        erem_all = jnp.exp(last_all - gam_all)
        b_i = [beta_all[:, LANE_GDN_B + h:LANE_GDN_B + h + 1] for h in heads]
        egam_i = [egam_all[:, LANE_GDN_A + h:LANE_GDN_A + h + 1] for h in heads]
        dec = [jnp.exp(jnp.where(causal, gam_all[:, LANE_GDN_A + h:LANE_GDN_A + h + 1]
                                 - gam_t[LANE_GDN_A + h:LANE_GDN_A + h + 1, :], -jnp.inf)) for h in heads]
        kh, kq = [], []
        for h in heads:
            qr = q_ref[0, rows, hcols[h]].astype(F32)
            kr = k_ref[0, rows, hcols[h]].astype(F32)
            qn = qr * (lax.rsqrt(jnp.sum(qr * qr, axis=-1, keepdims=True) + EPS) * qscale)
            kn = kr * lax.rsqrt(jnp.sum(kr * kr, axis=-1, keepdims=True) + EPS)
            kh.append(kn)
            kq.append(jnp.concatenate([kn.astype(BF16), qn.astype(BF16)], axis=0))
        kqk = [_dot_nt(kq[h], kq[h][0:CHUNK]) for h in heads]
        x = [jnp.where(strict, kqk[h][0:CHUNK] * dec[h] * (-b_i[h]), 0.0) for h in heads]
        xd = [jnp.where(blockdiag, x[h], 0.0) for h in heads]
        xn = [_split_bf16(jnp.where(blockdiag, 0.0, x[h])) for h in heads]
        z = [_split_bf16(xd[h]) for h in heads]
        p = [eye + xd[h] for h in heads]
        for _ in range(3):
            zf = [_dot_x3(z[h], z[h]) for h in heads]
            z = [_split_bf16(zf[h]) for h in heads]
            p = [p[h] + _dot_x3(_split_bf16(p[h]), z[h]) for h in heads]
        t0 = [_split_bf16(p[h]) for h in heads]
        m = [_dot_x3(t0[h], xn[h]) for h in heads]
        ms = [_split_bf16(m[h]) for h in heads]
        m2 = [_split_bf16(_dot_x3(ms[h], ms[h])) for h in heads]
        ipm = [eye + m[h] for h in heads]
        qm = [ipm[h] + _dot_x3(_split_bf16(ipm[h]), m2[h]) for h in heads]
        tinv = [_split_bf16(_dot_x3(_split_bf16(qm[h]), t0[h])) for h in heads]
        rhs = [_split_bf16(jnp.concatenate([v_ref[0, rows, hcols[h]].astype(F32) * b_i[h],
                                            kh[h] * (b_i[h] * egam_i[h])], axis=1)) for h in heads]
        sol = [_dot_x3(tinv[h], rhs[h]) for h in heads]
        s = [s_scr[h] for h in heads]
        sb = [s[h].astype(BF16) for h in heads]
        wq = [jnp.concatenate([sol[h][:, GDN_DV:].astype(BF16), kq[h][CHUNK:]], axis=0) for h in heads]
        ws = [_dot(wq[h], sb[h]) for h in heads]
        ub = [(sol[h][:, :GDN_DV] - ws[h][0:CHUNK]).astype(BF16) for h in heads]
        qk = [(kqk[h][CHUNK:] * dec[h]).astype(BF16) for h in heads]
        o = [egam_i[h] * ws[h][CHUNK:] + _dot(qk[h], ub[h]) for h in heads]
        for h in heads:
            lg = LANE_GDN_A + h
            kdec = (kh[h] * erem_all[:, lg:lg + 1]).astype(BF16)
            s_scr[h] = elast_all[:, lg:lg + 1] * s[h] + _dot_tn(kdec, ub[h])
        for h in heads:
            zh = z_ref[0, rows, hcols[h]].astype(F32)
            on = (o[h] * lax.rsqrt(jnp.mean(o[h] * o[h], axis=-1, keepdims=True) + EPS)
                  * gn_ref[...] * _silu(zh))
            o_ref[0, rows, hcols[h]] = on.astype(o_ref.dtype)
        return carry

    lax.fori_loop(0, nchunk, chunk, 0)

    @pl.when(t == pl.num_programs(1) - 1)
    def _():
        sout_ref[0] = s_scr[...]


def _gdn_call(xc, raw, small, s0, aneg_l, dtb_l, gnorm, *, tc):
    b, t, _ = xc.shape
    kern = functools.partial(_gdn_kernel, nchunk=tc // CHUNK)
    wide = lambda j: pl.BlockSpec((1, tc, GDN_QK), lambda i, m: (i, m, j))
    state = pl.BlockSpec((1, GDN_HEADS, GDN_DK, GDN_DV), lambda i, m: (i, 0, 0, 0))
    return pl.pallas_call(
        kern,
        grid=(b, t // tc),
        in_specs=[wide(0), wide(1), wide(2), wide(0),
                  pl.BlockSpec((1, tc, LANES), lambda i, m: (i, m, 0)),
                  state,
                  _const_spec((1, LANES)), _const_spec((1, LANES)), _const_spec((1, GDN_DV))],
        out_specs=[pl.BlockSpec((1, tc, GDN_V), lambda i, m: (i, m, 0)), state],
        out_shape=[jax.ShapeDtypeStruct((b, t, GDN_V), BF16),
                   jax.ShapeDtypeStruct((b, GDN_HEADS, GDN_DK, GDN_DV), F32)],
        scratch_shapes=[pltpu.VMEM((GDN_HEADS, GDN_DK, GDN_DV), F32)],
        compiler_params=pltpu.CompilerParams(
            dimension_semantics=("arbitrary", "arbitrary"),
            vmem_limit_bytes=VMEM_LIMIT_BYTES),
        name="gdn_scan",
    )(xc, xc, xc, raw, small, s0, aneg_l, dtb_l, gnorm)


def _ssd_kernel(xlo_ref, xhi_ref, bm_ref, cm_ref, zlo_ref, zhi_ref, sm_ref, h0_ref,
                aneg_ref, dtb_ref, dexp_ref, nrm_ref,
                y_ref, hout_ref, h_scr, *, nchunk):
    t = pl.program_id(1)

    @pl.when(t == 0)
    def _():
        h_scr[...] = h0_ref[0]

    pair_w = 2 * SSD_HEADDIM
    row = lax.broadcasted_iota(jnp.int32, (CHUNK, pair_w), 0)
    lane = lax.broadcasted_iota(jnp.int32, (CHUNK, pair_w), 1)
    causal2 = row >= (lane % CHUNK)
    first = lane < SSD_HEADDIM
    rr = lax.broadcasted_iota(jnp.int32, (pair_w, pair_w), 0)
    cc = lax.broadcasted_iota(jnp.int32, (pair_w, pair_w), 1)
    pair_diag = (rr // SSD_HEADDIM) == (cc // SSD_HEADDIM)
    row_c = lax.broadcasted_iota(jnp.int32, (CHUNK, CHUNK), 0)
    col_c = lax.broadcasted_iota(jnp.int32, (CHUNK, CHUNK), 1)
    tril_b = (row_c >= col_c).astype(BF16)
    half = SSD_GROUPS // 2
    groups = range(SSD_GROUPS)
    pairs = range(SSD_HEADS // 2)
    ppg = SSD_HPG // 2

    def chunk(c, carry):
        r0 = pl.multiple_of(c * CHUNK, CHUNK)
        rows = pl.ds(r0, CHUNK)
        sm = sm_ref[0, rows, :]
        dt_all = _softplus(sm + dtb_ref[...])
        da_hi, da_lo = _split_bf16(dt_all * aneg_ref[...])
        gam_all = _dot(tril_b, da_hi) + _dot(tril_b, da_lo)
        gam_t2 = jnp.concatenate([gam_all, pltpu.roll(gam_all, LANES - 1, 1)], axis=0).T
        dt_t2 = jnp.concatenate([dt_all, pltpu.roll(dt_all, LANES - 1, 1)], axis=0).T
        gam_i2, dt_i2 = [], []
        for p in pairs:
            l = LANE_SSD_DT + 2 * p
            gam_i2.append(jnp.where(first, gam_all[:, l:l + 1], gam_all[:, l + 1:l + 2]))
            dt_i2.append(jnp.where(first, dt_all[:, l:l + 1], dt_all[:, l + 1:l + 2]))
        dec2 = [jnp.exp(jnp.where(causal2, gam_i2[p] - gam_t2[LANE_SSD_DT + 2 * p:LANE_SSD_DT + 2 * p + 1, :],
                                  -jnp.inf)) for p in pairs]
        egam2 = [jnp.exp(gam_i2[p]) for p in pairs]
        last2 = [gam_i2[p][CHUNK - 1:CHUNK, :] for p in pairs]
        wj2 = [jnp.exp(last2[p] - gam_i2[p]) * dt_i2[p] for p in pairs]
        elast2 = [jnp.exp(last2[p]) for p in pairs]

        xb, xg, bmb, cmb = [], [], [], []
        for g in groups:
            x_ref = xlo_ref if g < half else xhi_ref
            xb.append(x_ref[0, rows, (g % half) * SSD_GSZ:(g % half + 1) * SSD_GSZ])
            xg.append(xb[g].astype(F32))
            bmb.append(bm_ref[0, rows, g * SSD_STATE:(g + 1) * SSD_STATE])
            cmb.append(cm_ref[0, rows, g * SSD_STATE:(g + 1) * SSD_STATE])
        cb2 = [_dot_nt(cmb[g], jnp.concatenate([bmb[g], bmb[g]], axis=0)) for g in groups]
        ypair = []
        for p in pairs:
            g = p // ppg
            xp = xb[g][:, (p % ppg) * pair_w:(p % ppg + 1) * pair_w]
            xbd = jnp.where(pair_diag, jnp.concatenate([xp, xp], axis=0), 0.0)
            w2 = cb2[g] * dec2[p] * dt_t2[LANE_SSD_DT + 2 * p:LANE_SSD_DT + 2 * p + 1, :]
            ypair.append(_dot(w2.astype(BF16), xbd))
        hg = [h_scr[g] for g in groups]
        ch = [_dot(cmb[g], hg[g].astype(BF16)) for g in groups]
        for g in groups:
            ps = range(g * ppg, (g + 1) * ppg)
            wj = jnp.concatenate([wj2[p] for p in ps], axis=1)
            elast = jnp.concatenate([elast2[p] for p in ps], axis=1)
            h_scr[g] = elast * hg[g] + _dot_tn(bmb[g], (xg[g] * wj).astype(BF16))
        for g in groups:
            ps = range(g * ppg, (g + 1) * ppg)
            z_ref = zlo_ref if g < half else zhi_ref
            ocols = slice(g * SSD_GSZ, (g + 1) * SSD_GSZ)
            egam = jnp.concatenate([egam2[p] for p in ps], axis=1)
            y = jnp.concatenate([ypair[p] for p in ps], axis=1) + ch[g] * egam
            y = y + dexp_ref[:, ocols] * xg[g]
            yz = y * _silu(z_ref[0, rows, (g % half) * SSD_GSZ:(g % half + 1) * SSD_GSZ].astype(F32))
            yn = yz * lax.rsqrt(jnp.mean(yz * yz, axis=-1, keepdims=True) + EPS) * nrm_ref[:, ocols]
            y_ref[0, rows, ocols] = yn.astype(y_ref.dtype)
        return carry

    lax.fori_loop(0, nchunk, chunk, 0)

    @pl.when(t == pl.num_programs(1) - 1)
    def _():
        hout_ref[0] = h_scr[...]


def _ssd_call(xc, raw, small, h0g, aneg_l, dtb_l, dexp, nrm, *, tc):
    b, t, _ = xc.shape
    kern = functools.partial(_ssd_kernel, nchunk=tc // CHUNK)
    wide = lambda j: pl.BlockSpec((1, tc, 1024), lambda i, m: (i, m, j))
    state = pl.BlockSpec((1, SSD_GROUPS, SSD_STATE, SSD_GSZ), lambda i, m: (i, 0, 0, 0))
    return pl.pallas_call(
        kern,
        grid=(b, t // tc),
        in_specs=[wide(3), wide(4), wide(5), wide(6), wide(1), wide(2),
                  pl.BlockSpec((1, tc, LANES), lambda i, m: (i, m, 0)),
                  state,
                  _const_spec((1, LANES)), _const_spec((1, LANES)),
                  _const_spec((1, SSD_INNER)), _const_spec((1, SSD_INNER))],
        out_specs=[pl.BlockSpec((1, tc, SSD_INNER), lambda i, m: (i, m, 0)), state],
        out_shape=[jax.ShapeDtypeStruct((b, t, SSD_INNER), BF16),
                   jax.ShapeDtypeStruct((b, SSD_GROUPS, SSD_STATE, SSD_GSZ), F32)],
        scratch_shapes=[pltpu.VMEM((SSD_GROUPS, SSD_STATE, SSD_GSZ), F32)],
        compiler_params=pltpu.CompilerParams(
            dimension_semantics=("arbitrary", "arbitrary"),
            vmem_limit_bytes=VMEM_LIMIT_BYTES),
        name="ssd_scan",
    )(xc, xc, xc, xc, raw, raw, small, h0g, aneg_l, dtb_l, dexp, nrm)


def _mlp_kernel(x_ref, og_ref, ys_ref, ga_ref, gb_ref, mod_ref,
                wbg_ref, wbs_ref, wo_ref, wn2_ref, wup_ref, cw_ref, cb_ref, wdn_ref, wnf_ref, st_ref,
                out_ref, cst_ref, tail_scr, sa_scr, sv_scr, *, nseq, tl, cf):
    m = pl.program_id(1)
    for s in range(nseq):
        @pl.when(m == 0)
        def _():
            tail_scr[s] = st_ref[s]

        md = mod_ref[s]
        mixed = (_sigmoid(ga_ref[s].astype(F32)) * _dot(og_ref[s], wbg_ref[...])
                 + _sigmoid(gb_ref[s].astype(F32)) * _dot(ys_ref[s], wbs_ref[...]))
        x1 = x_ref[s] + md[2:3] * _dot(mixed.astype(BF16), wo_ref[...])
        ms = jnp.mean(x1 * x1, axis=-1, keepdims=True)
        h2 = ((x1 * lax.rsqrt(ms + EPS) * wn2_ref[...]) * (1.0 + md[4:5]) + md[3:4]).astype(BF16)

        def conv(stage, u, cols):
            stage[0:SUBLANES] = tail_scr[s, :, cols]
            stage[SUBLANES:SUBLANES + tl] = u
            y = cb_ref[:, cols] + cw_ref[2:3, cols] * stage[8:8 + tl]
            y = y + cw_ref[1:2, cols] * stage[7:7 + tl]
            y = y + cw_ref[0:1, cols] * stage[6:6 + tl]
            tail_scr[s, :, cols] = stage[tl:tl + SUBLANES]
            return y

        def up(j):
            ca = slice(j * cf, (j + 1) * cf)
            cv = slice(D_FF + j * cf, D_FF + (j + 1) * cf)
            return ca, cv, _dot(h2, wup_ref[:, ca]), _dot(h2, wup_ref[:, cv])

        acc = jnp.zeros((tl, D_MODEL), F32)
        nxt = up(0)
        for j in range(D_FF // cf):
            ca, cv, pa, pv = nxt
            if j + 1 < D_FF // cf:
                nxt = up(j + 1)
            ua = conv(sa_scr.at[j % 2], pa, ca)
            uv = conv(sv_scr.at[j % 2], pv, cv)
            act = (_silu(ua) * uv).astype(BF16)
            acc = acc + _dot(act, wdn_ref[ca, :])
        x2 = x1 + md[5:6] * acc
        ms2 = jnp.mean(x2 * x2, axis=-1, keepdims=True)
        out_ref[s] = x2 * lax.rsqrt(ms2 + EPS) * wnf_ref[...]
        cst_ref[s] = tail_scr[s]


def _mlp_call(x, og, ys, raw, mod8, wbg, wbs, wo, wn2, wup, cw8, cb, wdn, wnf, st8, *, nseq, tl, cf):
    b, t, _ = x.shape
    kern = functools.partial(_mlp_kernel, nseq=nseq, tl=tl, cf=cf)
    tok = lambda width, j: pl.BlockSpec((nseq, tl, width), lambda i, m: (i, m, j))
    per_seq = lambda width: pl.BlockSpec((nseq, SUBLANES, width), lambda i, m: (i, 0, 0))

    def resident(shape):
        zeros = (0,) * len(shape)
        return pl.BlockSpec(shape, lambda i, m: zeros, pipeline_mode=pl.Buffered(1))

    return pl.pallas_call(
        kern,
        grid=(b // nseq, t // tl),
        in_specs=[tok(D_MODEL, 0), tok(GDN_V, 0), tok(SSD_INNER, 0), tok(D_MODEL, 3), tok(D_MODEL, 4),
                  per_seq(D_MODEL),
                  resident((GDN_V, D_MODEL)), resident((SSD_INNER, D_MODEL)), resident((D_MODEL, D_MODEL)),
                  resident((1, D_MODEL)), resident((D_MODEL, 2 * D_FF)),
                  resident((SUBLANES, 2 * D_FF)), resident((1, 2 * D_FF)),
                  resident((D_FF, D_MODEL)), resident((1, D_MODEL)),
                  per_seq(2 * D_FF)],
        out_specs=[tok(D_MODEL, 0), per_seq(2 * D_FF)],
        out_shape=[jax.ShapeDtypeStruct((b, t, D_MODEL), F32),
                   jax.ShapeDtypeStruct((b, SUBLANES, 2 * D_FF), F32)],
        scratch_shapes=[pltpu.VMEM((nseq, SUBLANES, 2 * D_FF), F32),
                        pltpu.VMEM((2, tl + SUBLANES, cf), F32),
                        pltpu.VMEM((2, tl + SUBLANES, cf), F32)],
        compiler_params=pltpu.CompilerParams(
            dimension_semantics=("arbitrary", "arbitrary"),
            vmem_limit_bytes=VMEM_LIMIT_BYTES),
        name="merge_ffn",
    )(x, og, ys, raw, raw, mod8, wbg, wbs, wo, wn2, wup, cw8, cb, wdn, wnf, st8)


def _lane_vec(values, offset):
    return jnp.zeros((1, LANES), F32).at[0, offset:offset + values.shape[0]].set(values.astype(F32))


def _prep_params(w_norm_mix, w_in, w_conv_mix, b_conv_mix, gdn_a_log, gdn_dt_bias, gdn_norm,
                 ssd_a_log, ssd_dt_bias, ssd_d, ssd_norm, w_branch_gdn, w_branch_ssd, w_out,
                 w_norm_ffn, w_ffn_up, w_ffn_conv, b_ffn_conv, w_ffn_down, w_norm_final, tn):
    o_small = CONV_CH + GDN_V + SSD_INNER
    o_merge = o_small + SMALL_CH
    w_small = jnp.pad(w_in[:, o_small:o_merge], ((0, 0), (0, LANES - SMALL_CH))).astype(BF16)
    w_all = jnp.concatenate([w_in[:, :o_small], w_in[:, o_merge:]], axis=1).astype(BF16)
    return dict(
        w_norm_mix=w_norm_mix.reshape(1, D_MODEL),
        w_all=w_all,
        w_small=w_small,
        cw_mix=jnp.pad(w_conv_mix, ((0, SUBLANES - CONV_W), (0, 0))),
        cb_mix=b_conv_mix.reshape(1, CONV_CH),
        gdn_aneg=_lane_vec(-jnp.exp(gdn_a_log.astype(F32)), LANE_GDN_A),
        gdn_dtb=_lane_vec(gdn_dt_bias, LANE_GDN_A),
        gdn_norm=gdn_norm.reshape(1, GDN_DV).astype(F32),
        ssd_aneg=_lane_vec(-jnp.exp(ssd_a_log.astype(F32)), LANE_SSD_DT),
        ssd_dtb=_lane_vec(ssd_dt_bias, LANE_SSD_DT),
        ssd_dexp=jnp.repeat(ssd_d.astype(F32), SSD_HEADDIM).reshape(1, SSD_INNER),
        ssd_norm=ssd_norm.reshape(1, SSD_INNER).astype(F32),
        wbg=w_branch_gdn.astype(BF16), wbs=w_branch_ssd.astype(BF16), wo=w_out.astype(BF16),
        w_norm_ffn=w_norm_ffn.reshape(1, D_MODEL),
        wup=w_ffn_up.astype(BF16),
        cw_ffn=jnp.pad(w_ffn_conv, ((0, SUBLANES - FFN_CONV_W), (0, 0))),
        cb_ffn=b_ffn_conv.reshape(1, 2 * D_FF),
        wdn=w_ffn_down.astype(BF16),
        w_norm_final=w_norm_final.reshape(1, D_MODEL),
    )


def _tiles(b, t):
    if t <= CHUNK:
        return dict(nseq=b, tl_in=t, tl_mlp=t, tc=t)
    tl_in = min(t, 1024)
    tl_mlp = min(t, 256)
    tc = min(t, 256)
    return dict(nseq=1, tl_in=tl_in, tl_mlp=tl_mlp, tc=tc)


def _trunk(x, mod, conv_mix0, s_gdn0, s_ssd0, conv_ffn0, p, tn, cf):
    b, t, _ = x.shape
    tl = _tiles(b, t)
    mod8 = jnp.pad(mod.reshape(b, 6, D_MODEL), ((0, 0), (0, SUBLANES - 6), (0, 0)))
    st_mix = jnp.pad(conv_mix0, ((0, 0), (SUBLANES - (CONV_W - 1), 0), (0, 0)))
    xc, raw, small, cst_mix = _inproj_call(
        x, mod8, p["w_norm_mix"], p["w_all"], p["w_small"], p["cw_mix"], p["cb_mix"], st_mix,
        nseq=tl["nseq"], tl=tl["tl_in"], tn=tn)
    og, s_gdn = _gdn_call(xc, raw, small, s_gdn0, p["gdn_aneg"], p["gdn_dtb"], p["gdn_norm"], tc=tl["tc"])
    h0g = s_ssd0.reshape(b, SSD_GROUPS, SSD_HPG, SSD_STATE, SSD_HEADDIM)
    h0g = jnp.swapaxes(h0g, 2, 3).reshape(b, SSD_GROUPS, SSD_STATE, SSD_GSZ)
    ys, hg = _ssd_call(xc, raw, small, h0g, p["ssd_aneg"], p["ssd_dtb"], p["ssd_dexp"], p["ssd_norm"],
                       tc=tl["tc"])
    s_ssd = jnp.swapaxes(hg.reshape(b, SSD_GROUPS, SSD_STATE, SSD_HPG, SSD_HEADDIM), 2, 3)
    s_ssd = s_ssd.reshape(b, SSD_HEADS, SSD_STATE, SSD_HEADDIM)
    st_ffn = jnp.pad(conv_ffn0, ((0, 0), (SUBLANES - (FFN_CONV_W - 1), 0), (0, 0)))
    y, cst_ffn = _mlp_call(
        x, og, ys, raw, mod8, p["wbg"], p["wbs"], p["wo"], p["w_norm_ffn"], p["wup"],
        p["cw_ffn"], p["cb_ffn"], p["wdn"], p["w_norm_final"], st_ffn,
        nseq=tl["nseq"], tl=tl["tl_mlp"], cf=cf)
    return (y, cst_mix[:, -1, SUBLANES - (CONV_W - 1):], s_gdn, s_ssd,
            cst_ffn[:, SUBLANES - (FFN_CONV_W - 1):])


def kernel(x_prompt, x_sample, state_conv_mix, state_gdn, state_ssd, state_conv_ffn, c_prompt, c_sample,
           w_ada, b_ada, w_norm_mix, w_in, w_conv_mix, b_conv_mix, gdn_a_log, gdn_dt_bias, gdn_norm,
           ssd_a_log, ssd_dt_bias, ssd_d, ssd_norm, w_branch_gdn, w_branch_ssd, w_out,
           w_norm_ffn, w_ffn_up, w_ffn_conv, b_ffn_conv, w_ffn_down, w_norm_final):
    depth = w_in.shape[0]
    assert depth == 1, "single-layer trunk"
    nb = x_prompt.shape[0]
    ndec = x_sample.shape[0]
    tn = 512
    cf = 256
    l = 0
    p = _prep_params(w_norm_mix[l], w_in[l], w_conv_mix[l], b_conv_mix[l], gdn_a_log[l], gdn_dt_bias[l],
                     gdn_norm[l], ssd_a_log[l], ssd_dt_bias[l], ssd_d[l], ssd_norm[l], w_branch_gdn[l],
                     w_branch_ssd[l], w_out[l], w_norm_ffn[l], w_ffn_up[l], w_ffn_conv[l], b_ffn_conv[l],
                     w_ffn_down[l], w_norm_final, tn)
    c_all = jnp.concatenate([c_prompt, c_sample], axis=0)
    rows = -(-c_all.shape[0] // SUBLANES) * SUBLANES
    mod = _mod_call(jnp.pad(c_all, ((0, rows - c_all.shape[0]), (0, 0))), w_ada[l], b_ada[l])
    dtype = x_prompt.dtype
    zeros = lambda *shape: jnp.zeros(shape, dtype)
    yp, a1, a2, a3, a4 = _trunk(
        x_prompt, mod[:nb],
        zeros(nb, CONV_W - 1, CONV_CH), zeros(nb, GDN_HEADS, GDN_DK, GDN_DV),
        zeros(nb, SSD_HEADS, SSD_STATE, SSD_HEADDIM), zeros(nb, FFN_CONV_W - 1, 2 * D_FF), p, tn, cf)
    ys, b1, b2, b3, b4 = _trunk(
        x_sample, mod[nb:nb + ndec],
        state_conv_mix[l], state_gdn[l], state_ssd[l], state_conv_ffn[l], p, tn, cf)
    return (yp, ys, a1[None], a2[None], a3[None], a4[None], b1[None], b2[None], b3[None], b4[None])
```

```python
import functools

import jax
import jax.numpy as jnp
from jax import lax
from jax.experimental import pallas as pl
from jax.experimental.pallas import tpu as pltpu

F32 = jnp.float32
BF16 = jnp.bfloat16
HIGHEST = lax.Precision.HIGHEST

D_MODEL = 1024
CHUNK = 64
EPS = 1e-6
CONV_W = 4
FFN_CONV_W = 3
GDN_HEADS = 8
GDN_DK = 128
GDN_DV = 128
GDN_QK = GDN_HEADS * GDN_DK
GDN_V = GDN_HEADS * GDN_DV
SSD_INNER = 2 * D_MODEL
SSD_HEADDIM = 64
SSD_HEADS = SSD_INNER // SSD_HEADDIM
SSD_STATE = 128
SSD_GROUPS = 8
SSD_HPG = SSD_HEADS // SSD_GROUPS
SSD_GSZ = SSD_INNER // SSD_GROUPS
SSD_BC = SSD_GROUPS * SSD_STATE
D_FF = 2816
CONV_CH = 2 * GDN_QK + GDN_V + SSD_INNER + 2 * SSD_BC
RAW_CH = GDN_V + SSD_INNER + 2 * D_MODEL
SMALL_CH = 2 * GDN_HEADS + SSD_HEADS
LANES = 128
SUBLANES = 8
LANE_GDN_B = 0
LANE_GDN_A = GDN_HEADS
LANE_SSD_DT = 2 * GDN_HEADS
VMEM_LIMIT_BYTES = 56 * 1024 * 1024


def _sigmoid(x):
    return 1.0 / (1.0 + jnp.exp(-x))


def _silu(x):
    return x * _sigmoid(x)


def _softplus(x):
    return jnp.maximum(x, 0.0) + jnp.log(1.0 + jnp.exp(-jnp.abs(x)))


def _dot(a, b, precision=None):
    return jnp.dot(a, b, preferred_element_type=F32, precision=precision)


def _dot_nt(a, b):
    return lax.dot_general(a, b, (((1,), (1,)), ((), ())), preferred_element_type=F32)


def _dot_tn(a, b):
    return lax.dot_general(a, b, (((0,), (0,)), ((), ())), preferred_element_type=F32)


def _split_bf16(x):
    hi = x.astype(BF16)
    lo = (x - hi.astype(F32)).astype(BF16)
    return hi, lo


def _const_spec(shape):
    zeros = (0,) * len(shape)
    return pl.BlockSpec(shape, lambda *_: zeros)


def _mod_kernel(c_ref, w_ref, b_ref, o_ref):
    o_ref[...] = _dot(_silu(c_ref[...]), w_ref[...], HIGHEST) + b_ref[...]


def _mod_call(c_pad, w_ada, b_ada):
    rows = c_pad.shape[0]
    n = w_ada.shape[1]
    tn = 512
    return pl.pallas_call(
        _mod_kernel,
        grid=(n // tn,),
        in_specs=[pl.BlockSpec((rows, D_MODEL), lambda j: (0, 0)),
                  pl.BlockSpec((D_MODEL, tn), lambda j: (0, j)),
                  pl.BlockSpec((1, tn), lambda j: (0, j))],
        out_specs=pl.BlockSpec((rows, tn), lambda j: (0, j)),
        out_shape=jax.ShapeDtypeStruct((rows, n), F32),
        name="adaln_mod",
    )(c_pad, w_ada, b_ada.reshape(1, n))


COL_TILE = 1024
COL_SUB = 512
ROW_SUB_CONV = 128
ROW_SUB_RAW = 256


def _inproj_kernel(*refs, nseq, tl, conv, ncs):
    if conv:
        (x_ref, mod_ref, wn_ref, w_ref, ws_ref, cw_ref, cb_ref, st_ref,
         out_ref, sm_ref, cst_ref, h_scr, tail_scr, stage_scr) = refs
    else:
        x_ref, mod_ref, wn_ref, w_ref, out_ref, h_scr = refs
    m = pl.program_id(1)
    n = pl.program_id(2)

    @pl.when(n == 0)
    def _():
        for s in range(nseq):
            x = x_ref[s]
            ms = jnp.mean(x * x, axis=-1, keepdims=True)
            md = mod_ref[s]
            h = (x * lax.rsqrt(ms + EPS) * wn_ref[...]) * (1.0 + md[1:2]) + md[0:1]
            hb = h.astype(BF16)
            h_scr[s * tl:(s + 1) * tl, :] = hb
            if conv:
                sm_ref[s] = _dot(hb, ws_ref[...])

    if conv:
        for s in range(nseq):
            @pl.when(m == 0)
            def _():
                tail_scr[n, s] = st_ref[s]

    sr = min(tl, ROW_SUB_CONV if conv else ROW_SUB_RAW)
    for cj in range(ncs):
        cs = slice(cj * COL_SUB, (cj + 1) * COL_SUB)
        for s in range(nseq):
            for lo in range(0, tl, sr):
                acc = _dot(h_scr[s * tl + lo:s * tl + lo + sr, :], w_ref[cj])
                if not conv:
                    out_ref[s, lo:lo + sr, cs] = acc.astype(out_ref.dtype)
                    continue
                stg = stage_scr.at[s, cj % 2]
                if lo == 0:
                    stg[0:SUBLANES] = tail_scr[n, s, :, cs]
                stg[SUBLANES + lo:SUBLANES + lo + sr] = acc
                ext = stg[lo:lo + sr + SUBLANES]
                y = cb_ref[:, cs] + cw_ref[3:4, cs] * ext[SUBLANES:]
                y = y + cw_ref[2:3, cs] * pltpu.roll(ext, 1, 0)[SUBLANES:]
                y = y + cw_ref[1:2, cs] * pltpu.roll(ext, 2, 0)[SUBLANES:]
                y = y + cw_ref[0:1, cs] * pltpu.roll(ext, 3, 0)[SUBLANES:]
                out_ref[s, lo:lo + sr, cs] = _silu(y).astype(out_ref.dtype)
                if lo + sr == tl:
                    new_tail = stg[tl:tl + SUBLANES]
                    tail_scr[n, s, :, cs] = new_tail
                    cst_ref[s, 0, :, cs] = new_tail


def _inproj_call(x, mod8, w_norm, w, conv_args, *, nseq, tl):
    b, t, _ = x.shape
    tn = w.shape[1] * w.shape[3]
    cols = w.shape[0] * tn
    conv = conv_args is not None
    grid = (b // nseq, t // tl, cols // tn)
    kern = functools.partial(_inproj_kernel, nseq=nseq, tl=tl, conv=conv, ncs=tn // COL_SUB)
    in_specs = [
        pl.BlockSpec((nseq, tl, D_MODEL), lambda i, m, n: (i, m, 0)),
        pl.BlockSpec((nseq, SUBLANES, D_MODEL), lambda i, m, n: (i, 0, 0)),
        pl.BlockSpec((1, D_MODEL), lambda i, m, n: (0, 0)),
        pl.BlockSpec((None, tn // COL_SUB, D_MODEL, COL_SUB), lambda i, m, n: (n, 0, 0, 0)),
    ]
    out_specs = [pl.BlockSpec((None, nseq, tl, tn), lambda i, m, n: (n, i, m, 0))]
    out_shape = [jax.ShapeDtypeStruct((cols // tn, b, t, tn), BF16)]
    scratch = [pltpu.VMEM((nseq * tl, D_MODEL), BF16)]
    args = [x, mod8, w_norm, w]
    if conv:
        in_specs += [
            pl.BlockSpec((D_MODEL, LANES), lambda i, m, n: (0, 0)),
            pl.BlockSpec((SUBLANES, tn), lambda i, m, n: (0, n)),
            pl.BlockSpec((1, tn), lambda i, m, n: (0, n)),
            pl.BlockSpec((nseq, SUBLANES, tn), lambda i, m, n: (i, 0, n)),
        ]
        out_specs += [
            pl.BlockSpec((nseq, tl, LANES), lambda i, m, n: (i, m, 0)),
            pl.BlockSpec((nseq, 1, SUBLANES, tn), lambda i, m, n: (i, m, 0, n)),
        ]
        out_shape += [
            jax.ShapeDtypeStruct((b, t, LANES), F32),
            jax.ShapeDtypeStruct((b, t // tl, SUBLANES, cols), F32),
        ]
        scratch += [
            pltpu.VMEM((cols // tn, nseq, SUBLANES, tn), F32),
            pltpu.VMEM((nseq, 2, tl + SUBLANES, COL_SUB), F32),
        ]
        args += list(conv_args)
    return pl.pallas_call(
        kern,
        grid=grid,
        in_specs=in_specs,
        out_specs=out_specs,
        out_shape=out_shape,
        scratch_shapes=scratch,
        compiler_params=pltpu.CompilerParams(
            dimension_semantics=("arbitrary", "arbitrary", "arbitrary"),
            vmem_limit_bytes=VMEM_LIMIT_BYTES),
        name="inproj_conv" if conv else "inproj_raw",
    )(*args)


INV_BLOCK = 16


def _gdn_kernel(q_ref, k_ref, v_ref, z_ref, sm_ref, s0_ref, aneg_ref, dtb_ref, gn_ref,
                o_ref, sout_ref, s_scr, *, nchunk):
    t = pl.program_id(1)

    @pl.when(t == 0)
    def _():
        s_scr[...] = s0_ref[0]

    row_c = lax.broadcasted_iota(jnp.int32, (CHUNK, CHUNK), 0)
    col_c = lax.broadcasted_iota(jnp.int32, (CHUNK, CHUNK), 1)
    tril_b = (row_c >= col_c).astype(BF16)
    qscale = GDN_DK ** -0.5

    heads = range(GDN_HEADS)
    pairs = range(GDN_HEADS // 2)
    hcols = [slice(h * GDN_DK, (h + 1) * GDN_DK) for h in heads]
    row2 = lax.broadcasted_iota(jnp.int32, (CHUNK, 2 * CHUNK), 0)
    col2 = lax.broadcasted_iota(jnp.int32, (CHUNK, 2 * CHUNK), 1) % CHUNK
    first = lax.broadcasted_iota(jnp.int32, (CHUNK, 2 * CHUNK), 1) < CHUNK
    causal2 = row2 >= col2
    strict2 = row2 > col2
    eye2 = (row2 == col2).astype(F32)
    blockdiag2 = (row2 // INV_BLOCK) == (col2 // INV_BLOCK)
    rr = lax.broadcasted_iota(jnp.int32, (2 * CHUNK, 2 * CHUNK), 0)
    cc = lax.broadcasted_iota(jnp.int32, (2 * CHUNK, 2 * CHUNK), 1)
    pair_diag = (rr // CHUNK) == (cc // CHUNK)

    def pair_lhs(a):
        return jnp.concatenate([a[0], a[0], a[1]], axis=1)

    def pair_rhs(b):
        bd = lambda v: jnp.where(pair_diag, jnp.concatenate([v, v], axis=0), 0.0)
        return jnp.concatenate([bd(b[0]), bd(b[1]), bd(b[0])], axis=0)

    gsz = 4 if nchunk % 4 == 0 else (2 if nchunk % 2 == 0 else 1)

    def chunk_group(c, carry):
        cis = range(gsz)
        rows = [pl.ds(pl.multiple_of((c * gsz + ci) * CHUNK, CHUNK), CHUNK) for ci in cis]
        beta_all, gam_all, gam_t2, egam_all, elast_all, erem_all = [], [], [], [], [], []
        for ci in cis:
            sm = sm_ref[0, rows[ci], :]
            beta_all.append(_sigmoid(sm))
            g_hi, g_lo = _split_bf16(aneg_ref[...] * _softplus(sm + dtb_ref[...]))
            gam = _dot(tril_b, g_hi) + _dot(tril_b, g_lo)
            gam_all.append(gam)
            gam_t2.append(jnp.concatenate([gam, pltpu.roll(gam, LANES - 1, 1)], axis=0).T)
            egam_all.append(jnp.exp(gam))
            last = gam[CHUNK - 1:CHUNK, :]
            elast_all.append(jnp.exp(last))
            erem_all.append(jnp.exp(last - gam))
        units = [(ci, h) for ci in cis for h in heads]
        punits = [(ci, p) for ci in cis for p in pairs]
        hcol = lambda arr, lane: arr[:, lane:lane + 1]
        egam_i = {(ci, h): hcol(egam_all[ci], LANE_GDN_A + h) for ci, h in units}
        b_i = {(ci, h): hcol(beta_all[ci], LANE_GDN_B + h) for ci, h in units}
        b_i2, dec2 = {}, {}
        for ci, p in punits:
            lb, lg = LANE_GDN_B + 2 * p, LANE_GDN_A + 2 * p
            b_i2[ci, p] = jnp.where(first, hcol(beta_all[ci], lb), hcol(beta_all[ci], lb + 1))
            gam_i2 = jnp.where(first, hcol(gam_all[ci], lg), hcol(gam_all[ci], lg + 1))
            dec2[ci, p] = jnp.exp(jnp.where(causal2, gam_i2 - gam_t2[ci][lg:lg + 1, :], -jnp.inf))
        kh, kq = {}, {}
        for ci, h in units:
            qr = q_ref[0, rows[ci], hcols[h]].astype(F32)
            kr = k_ref[0, rows[ci], hcols[h]].astype(F32)
            qn = qr * (lax.rsqrt(jnp.sum(qr * qr, axis=-1, keepdims=True) + EPS) * qscale)
            kn = kr * lax.rsqrt(jnp.sum(kr * kr, axis=-1, keepdims=True) + EPS)
            kh[ci, h] = kn
            kq[ci, h] = jnp.concatenate([kn.astype(BF16), qn.astype(BF16)], axis=0)
        zk = jnp.zeros((CHUNK, GDN_DK), BF16)
        kqk2 = {}
        for ci, p in punits:
            ka, kb = kq[ci, 2 * p][0:CHUNK], kq[ci, 2 * p + 1][0:CHUNK]
            kbd = jnp.concatenate([jnp.concatenate([ka, zk], axis=1), jnp.concatenate([zk, kb], axis=1)], axis=0)
            kqk2[ci, p] = _dot_nt(jnp.concatenate([kq[ci, 2 * p], kq[ci, 2 * p + 1]], axis=1), kbd)
        x = {u: jnp.where(strict2, kqk2[u][0:CHUNK] * dec2[u] * (-b_i2[u]), 0.0) for u in punits}
        xd = {u: jnp.where(blockdiag2, x[u], 0.0) for u in punits}
        xn = {u: pair_rhs(_split_bf16(jnp.where(blockdiag2, 0.0, x[u]))) for u in punits}
        z = {u: _split_bf16(xd[u]) for u in punits}
        pm = {u: eye2 + xd[u] for u in punits}
        for _ in range(3):
            zf = {u: _dot(pair_lhs(z[u]), pair_rhs(z[u])) for u in punits}
            z = {u: _split_bf16(zf[u]) for u in punits}
            zr = {u: pair_rhs(z[u]) for u in punits}
            pm = {u: pm[u] + _dot(pair_lhs(_split_bf16(pm[u])), zr[u]) for u in punits}
        t0 = {u: _split_bf16(pm[u]) for u in punits}
        m = {u: _dot(pair_lhs(t0[u]), xn[u]) for u in punits}
        ms = {u: _split_bf16(m[u]) for u in punits}
        m2 = {u: pair_rhs(_split_bf16(_dot(pair_lhs(ms[u]), pair_rhs(ms[u])))) for u in punits}
        ipm = {u: eye2 + m[u] for u in punits}
        qm = {u: ipm[u] + _dot(pair_lhs(_split_bf16(ipm[u])), m2[u]) for u in punits}
        tinv = {u: pair_lhs(_split_bf16(_dot(pair_lhs(_split_bf16(qm[u])), pair_rhs(t0[u])))) for u in punits}
        rhs = {(ci, h): _split_bf16(jnp.concatenate(
            [v_ref[0, rows[ci], hcols[h]].astype(F32) * b_i[ci, h], kh[ci, h] * (b_i[ci, h] * egam_i[ci, h])],
            axis=1)) for ci, h in units}
        zr2 = jnp.zeros((CHUNK, 2 * GDN_DV), BF16)
        sol, qk2 = {}, {}
        for ci, p in punits:
            ra, rb = rhs[ci, 2 * p], rhs[ci, 2 * p + 1]
            bd = lambda i: jnp.concatenate([jnp.concatenate([ra[i], zr2], axis=1),
                                            jnp.concatenate([zr2, rb[i]], axis=1)], axis=0)
            sol2 = _dot(tinv[ci, p], jnp.concatenate([bd(0), bd(1), bd(0)], axis=0))
            sol[ci, 2 * p] = sol2[:, 0:2 * GDN_DV]
            sol[ci, 2 * p + 1] = sol2[:, 2 * GDN_DV:]
            qk2[ci, p] = (kqk2[ci, p][CHUNK:] * dec2[ci, p]).astype(BF16)
        kdec = {(ci, h): (kh[ci, h] * hcol(erem_all[ci], LANE_GDN_A + h)).astype(BF16) for ci, h in units}
        zu = jnp.zeros((CHUNK, GDN_DV), BF16)
        s = [s_scr[h] for h in heads]
        o = {}
        for ci in cis:
            sb = [s[h].astype(BF16) for h in heads]
            wq = [jnp.concatenate([sol[ci, h][:, GDN_DV:].astype(BF16), kq[ci, h][CHUNK:]], axis=0) for h in heads]
            ws = [_dot(wq[h], sb[h]) for h in heads]
            ub = [(sol[ci, h][:, :GDN_DV] - ws[h][0:CHUNK]).astype(BF16) for h in heads]
            for p in pairs:
                ubd = jnp.concatenate([jnp.concatenate([ub[2 * p], zu], axis=1),
                                       jnp.concatenate([zu, ub[2 * p + 1]], axis=1)], axis=0)
                o2 = _dot(qk2[ci, p], ubd)
                o[ci, 2 * p] = egam_i[ci, 2 * p] * ws[2 * p][CHUNK:] + o2[:, :GDN_DV]
                o[ci, 2 * p + 1] = egam_i[ci, 2 * p + 1] * ws[2 * p + 1][CHUNK:] + o2[:, GDN_DV:]
            s = [hcol(elast_all[ci], LANE_GDN_A + h) * s[h] + _dot_tn(kdec[ci, h], ub[h]) for h in heads]
        for h in heads:
            s_scr[h] = s[h]
        for ci, h in units:
            zh = z_ref[0, rows[ci], hcols[h]].astype(F32)
            on = (o[ci, h] * lax.rsqrt(jnp.mean(o[ci, h] * o[ci, h], axis=-1, keepdims=True) + EPS)
                  * gn_ref[...] * _silu(zh))
            o_ref[0, rows[ci], hcols[h]] = on.astype(o_ref.dtype)
        return carry

    lax.fori_loop(0, nchunk // gsz, chunk_group, 0)

    @pl.when(t == pl.num_programs(1) - 1)
    def _():
        sout_ref[0] = s_scr[...]


def _gdn_call(xc, raw, small, s0, aneg_l, dtb_l, gnorm, *, tc):
    _, b, t, _ = xc.shape
    kern = functools.partial(_gdn_kernel, nchunk=tc // CHUNK)
    wide = lambda j: pl.BlockSpec((None, 1, tc, COL_TILE), lambda i, m: (j, i, m, 0))
    state = pl.BlockSpec((1, GDN_HEADS, GDN_DK, GDN_DV), lambda i, m: (i, 0, 0, 0))
    return pl.pallas_call(
        kern,
        grid=(b, t // tc),
        in_specs=[wide(0), wide(1), wide(2), wide(0),
                  pl.BlockSpec((1, tc, LANES), lambda i, m: (i, m, 0)),
                  state,
                  _const_spec((1, LANES)), _const_spec((1, LANES)), _const_spec((1, GDN_DV))],
        out_specs=[pl.BlockSpec((1, tc, GDN_V), lambda i, m: (i, m, 0)), state],
        out_shape=[jax.ShapeDtypeStruct((b, t, GDN_V), BF16),
                   jax.ShapeDtypeStruct((b, GDN_HEADS, GDN_DK, GDN_DV), F32)],
        scratch_shapes=[pltpu.VMEM((GDN_HEADS, GDN_DK, GDN_DV), F32)],
        compiler_params=pltpu.CompilerParams(
            dimension_semantics=("arbitrary", "arbitrary"),
            vmem_limit_bytes=VMEM_LIMIT_BYTES),
        name="gdn_scan",
    )(xc, xc, xc, raw, small, s0, aneg_l, dtb_l, gnorm)


def _ssd_kernel(xlo_ref, xhi_ref, bm_ref, cm_ref, zlo_ref, zhi_ref, sm_ref, h0_ref,
                aneg_ref, dtb_ref, dexp_ref, nrm_ref,
                y_ref, hout_ref, h_scr, *, nchunk):
    t = pl.program_id(1)

    @pl.when(t == 0)
    def _():
        h_scr[...] = h0_ref[0]

    pair_w = 2 * SSD_HEADDIM
    row = lax.broadcasted_iota(jnp.int32, (CHUNK, pair_w), 0)
    lane = lax.broadcasted_iota(jnp.int32, (CHUNK, pair_w), 1)
    causal2 = row >= (lane % CHUNK)
    first = lane < SSD_HEADDIM
    rr = lax.broadcasted_iota(jnp.int32, (pair_w, pair_w), 0)
    cc = lax.broadcasted_iota(jnp.int32, (pair_w, pair_w), 1)
    pair_diag = (rr // SSD_HEADDIM) == (cc // SSD_HEADDIM)
    row_c = lax.broadcasted_iota(jnp.int32, (CHUNK, CHUNK), 0)
    col_c = lax.broadcasted_iota(jnp.int32, (CHUNK, CHUNK), 1)
    tril_b = (row_c >= col_c).astype(BF16)
    half = SSD_GROUPS // 2
    groups = range(SSD_GROUPS)
    pairs = range(SSD_HEADS // 2)
    ppg = SSD_HPG // 2

    def chunk(c, carry):
        r0 = pl.multiple_of(c * CHUNK, CHUNK)
        rows = pl.ds(r0, CHUNK)
        sm = sm_ref[0, rows, :]
        dt_all = _softplus(sm + dtb_ref[...])
        da_hi, da_lo = _split_bf16(dt_all * aneg_ref[...])
        gam_all = _dot(tril_b, da_hi) + _dot(tril_b, da_lo)
        gam_t2 = jnp.concatenate([gam_all, pltpu.roll(gam_all, LANES - 1, 1)], axis=0).T
        dt_t2 = jnp.concatenate([dt_all, pltpu.roll(dt_all, LANES - 1, 1)], axis=0).T
        gam_i2, dt_i2 = [], []
        for p in pairs:
            l = LANE_SSD_DT + 2 * p
            gam_i2.append(jnp.where(first, gam_all[:, l:l + 1], gam_all[:, l + 1:l + 2]))
            dt_i2.append(jnp.where(first, dt_all[:, l:l + 1], dt_all[:, l + 1:l + 2]))
        dec2 = [jnp.exp(jnp.where(causal2, gam_i2[p] - gam_t2[LANE_SSD_DT + 2 * p:LANE_SSD_DT + 2 * p + 1, :],
                                  -jnp.inf)) for p in pairs]
        egam2 = [jnp.exp(gam_i2[p]) for p in pairs]
        last2 = [gam_i2[p][CHUNK - 1:CHUNK, :] for p in pairs]
        wj2 = [jnp.exp(last2[p] - gam_i2[p]) * dt_i2[p] for p in pairs]
        elast2 = [jnp.exp(last2[p]) for p in pairs]

        xb, xg, bmb, cmb = [], [], [], []
        for g in groups:
            x_ref = xlo_ref if g < half else xhi_ref
            xb.append(x_ref[0, rows, (g % half) * SSD_GSZ:(g % half + 1) * SSD_GSZ])
            xg.append(xb[g].astype(F32))
            bmb.append(bm_ref[0, rows, g * SSD_STATE:(g + 1) * SSD_STATE])
            cmb.append(cm_ref[0, rows, g * SSD_STATE:(g + 1) * SSD_STATE])
        cb2 = [_dot_nt(cmb[g], jnp.concatenate([bmb[g], bmb[g]], axis=0)) for g in groups]
        ypair = []
        for p in pairs:
            g = p // ppg
            xp = xb[g][:, (p % ppg) * pair_w:(p % ppg + 1) * pair_w]
            xbd = jnp.where(pair_diag, jnp.concatenate([xp, xp], axis=0), 0.0)
            w2 = cb2[g] * dec2[p] * dt_t2[LANE_SSD_DT + 2 * p:LANE_SSD_DT + 2 * p + 1, :]
            ypair.append(_dot(w2.astype(BF16), xbd))
        hg = [h_scr[g] for g in groups]
        ch = [_dot(cmb[g], hg[g].astype(BF16)) for g in groups]
        for g in groups:
            ps = range(g * ppg, (g + 1) * ppg)
            wj = jnp.concatenate([wj2[p] for p in ps], axis=1)
            elast = jnp.concatenate([elast2[p] for p in ps], axis=1)
            h_scr[g] = elast * hg[g] + _dot_tn(bmb[g], (xg[g] * wj).astype(BF16))
        for g in groups:
            ps = range(g * ppg, (g + 1) * ppg)
            z_ref = zlo_ref if g < half else zhi_ref
            ocols = slice(g * SSD_GSZ, (g + 1) * SSD_GSZ)
            egam = jnp.concatenate([egam2[p] for p in ps], axis=1)
            y = jnp.concatenate([ypair[p] for p in ps], axis=1) + ch[g] * egam
            y = y + dexp_ref[:, ocols] * xg[g]
            yz = y * _silu(z_ref[0, rows, (g % half) * SSD_GSZ:(g % half + 1) * SSD_GSZ].astype(F32))
            yn = yz * lax.rsqrt(jnp.mean(yz * yz, axis=-1, keepdims=True) + EPS) * nrm_ref[:, ocols]
            y_ref[0, rows, ocols] = yn.astype(y_ref.dtype)
        return carry

    lax.fori_loop(0, nchunk, chunk, 0)

    @pl.when(t == pl.num_programs(1) - 1)
    def _():
        hout_ref[0] = h_scr[...]


def _ssd_call(xc, raw, small, h0g, aneg_l, dtb_l, dexp, nrm, *, tc):
    _, b, t, _ = xc.shape
    kern = functools.partial(_ssd_kernel, nchunk=tc // CHUNK)
    wide = lambda j: pl.BlockSpec((None, 1, tc, COL_TILE), lambda i, m: (j, i, m, 0))
    state = pl.BlockSpec((1, SSD_GROUPS, SSD_STATE, SSD_GSZ), lambda i, m: (i, 0, 0, 0))
    return pl.pallas_call(
        kern,
        grid=(b, t // tc),
        in_specs=[wide(3), wide(4), wide(5), wide(6), wide(1), wide(2),
                  pl.BlockSpec((1, tc, LANES), lambda i, m: (i, m, 0)),
                  state,
                  _const_spec((1, LANES)), _const_spec((1, LANES)),
                  _const_spec((1, SSD_INNER)), _const_spec((1, SSD_INNER))],
        out_specs=[pl.BlockSpec((1, tc, SSD_INNER), lambda i, m: (i, m, 0)), state],
        out_shape=[jax.ShapeDtypeStruct((b, t, SSD_INNER), BF16),
                   jax.ShapeDtypeStruct((b, SSD_GROUPS, SSD_STATE, SSD_GSZ), F32)],
        scratch_shapes=[pltpu.VMEM((SSD_GROUPS, SSD_STATE, SSD_GSZ), F32)],
        compiler_params=pltpu.CompilerParams(
            dimension_semantics=("arbitrary", "arbitrary"),
            vmem_limit_bytes=VMEM_LIMIT_BYTES),
        name="ssd_scan",
    )(xc, xc, xc, xc, raw, raw, small, h0g, aneg_l, dtb_l, dexp, nrm)


def _mlp_kernel(x_ref, og_ref, ys_ref, ga_ref, gb_ref, mod_ref,
                wbg_ref, wbs_ref, wo_ref, wn2_ref, wup_ref, cw_ref, cb_ref, wdn_ref, wnf_ref, st_ref,
                out_ref, cst_ref, tail_scr, sa_scr, sv_scr, *, nseq, tl, cf):
    m = pl.program_id(1)
    for s in range(nseq):
        @pl.when(m == 0)
        def _():
            tail_scr[s] = st_ref[s]

        md = mod_ref[s]
        mixed = (_sigmoid(ga_ref[s].astype(F32)) * _dot(og_ref[s], wbg_ref[...])
                 + _sigmoid(gb_ref[s].astype(F32)) * _dot(ys_ref[s], wbs_ref[...]))
        x1 = x_ref[s] + md[2:3] * _dot(mixed.astype(BF16), wo_ref[...])
        ms = jnp.mean(x1 * x1, axis=-1, keepdims=True)
        h2 = ((x1 * lax.rsqrt(ms + EPS) * wn2_ref[...]) * (1.0 + md[4:5]) + md[3:4]).astype(BF16)

        def conv(stage, u, cols):
            stage[0:SUBLANES] = tail_scr[s, :, cols]
            stage[SUBLANES:SUBLANES + tl] = u
            y = cb_ref[:, cols] + cw_ref[2:3, cols] * stage[8:8 + tl]
            y = y + cw_ref[1:2, cols] * stage[7:7 + tl]
            y = y + cw_ref[0:1, cols] * stage[6:6 + tl]
            tail_scr[s, :, cols] = stage[tl:tl + SUBLANES]
            return y

        def up(j):
            ca = slice(j * cf, (j + 1) * cf)
            cv = slice(D_FF + j * cf, D_FF + (j + 1) * cf)
            return ca, cv, _dot(h2, wup_ref[:, ca]), _dot(h2, wup_ref[:, cv])

        acc = jnp.zeros((tl, D_MODEL), F32)
        nxt = up(0)
        for j in range(D_FF // cf):
            ca, cv, pa, pv = nxt
            if j + 1 < D_FF // cf:
                nxt = up(j + 1)
            ua = conv(sa_scr.at[j % 2], pa, ca)
            uv = conv(sv_scr.at[j % 2], pv, cv)
            act = (_silu(ua) * uv).astype(BF16)
            acc = acc + _dot(act, wdn_ref[ca, :])
        x2 = x1 + md[5:6] * acc
        ms2 = jnp.mean(x2 * x2, axis=-1, keepdims=True)
        out_ref[s] = x2 * lax.rsqrt(ms2 + EPS) * wnf_ref[...]
        cst_ref[s] = tail_scr[s]


def _mlp_call(x, og, ys, raw, mod8, wbg, wbs, wo, wn2, wup, cw8, cb, wdn, wnf, st8, *, nseq, tl, cf):
    b, t, _ = x.shape
    kern = functools.partial(_mlp_kernel, nseq=nseq, tl=tl, cf=cf)
    tok = lambda width, j: pl.BlockSpec((nseq, tl, width), lambda i, m: (i, m, j))
    raw_tile = lambda j: pl.BlockSpec((None, nseq, tl, COL_TILE), lambda i, m: (j, i, m, 0))
    per_seq = lambda width: pl.BlockSpec((nseq, SUBLANES, width), lambda i, m: (i, 0, 0))

    def resident(shape):
        zeros = (0,) * len(shape)
        return pl.BlockSpec(shape, lambda i, m: zeros, pipeline_mode=pl.Buffered(1))

    return pl.pallas_call(
        kern,
        grid=(b // nseq, t // tl),
        in_specs=[tok(D_MODEL, 0), tok(GDN_V, 0), tok(SSD_INNER, 0), raw_tile(3), raw_tile(4),
                  per_seq(D_MODEL),
                  resident((GDN_V, D_MODEL)), resident((SSD_INNER, D_MODEL)), resident((D_MODEL, D_MODEL)),
                  resident((1, D_MODEL)), resident((D_MODEL, 2 * D_FF)),
                  resident((SUBLANES, 2 * D_FF)), resident((1, 2 * D_FF)),
                  resident((D_FF, D_MODEL)), resident((1, D_MODEL)),
                  per_seq(2 * D_FF)],
        out_specs=[tok(D_MODEL, 0), per_seq(2 * D_FF)],
        out_shape=[jax.ShapeDtypeStruct((b, t, D_MODEL), F32),
                   jax.ShapeDtypeStruct((b, SUBLANES, 2 * D_FF), F32)],
        scratch_shapes=[pltpu.VMEM((nseq, SUBLANES, 2 * D_FF), F32),
                        pltpu.VMEM((2, tl + SUBLANES, cf), F32),
                        pltpu.VMEM((2, tl + SUBLANES, cf), F32)],
        compiler_params=pltpu.CompilerParams(
            dimension_semantics=("arbitrary", "arbitrary"),
            vmem_limit_bytes=VMEM_LIMIT_BYTES),
        name="merge_ffn",
    )(x, og, ys, raw, raw, mod8, wbg, wbs, wo, wn2, wup, cw8, cb, wdn, wnf, st8)


def _lane_vec(values, offset):
    return jnp.zeros((1, LANES), F32).at[0, offset:offset + values.shape[0]].set(values.astype(F32))


def _prep_params(w_norm_mix, w_in, w_conv_mix, b_conv_mix, gdn_a_log, gdn_dt_bias, gdn_norm,
                 ssd_a_log, ssd_dt_bias, ssd_d, ssd_norm, w_branch_gdn, w_branch_ssd, w_out,
                 w_norm_ffn, w_ffn_up, w_ffn_conv, b_ffn_conv, w_ffn_down, w_norm_final):
    o_small = CONV_CH + GDN_V + SSD_INNER
    o_merge = o_small + SMALL_CH
    w_small = jnp.pad(w_in[:, o_small:o_merge], ((0, 0), (0, LANES - SMALL_CH))).astype(BF16)
    w_raw = jnp.concatenate([w_in[:, CONV_CH:o_small], w_in[:, o_merge:]], axis=1)
    col_tiled = lambda w: jnp.transpose(
        w.astype(BF16).reshape(D_MODEL, w.shape[1] // COL_TILE, COL_TILE // COL_SUB, COL_SUB), (1, 2, 0, 3))
    return dict(
        w_norm_mix=w_norm_mix.reshape(1, D_MODEL),
        w_conv=col_tiled(w_in[:, :CONV_CH]),
        w_raw=col_tiled(w_raw),
        w_small=w_small,
        cw_mix=jnp.pad(w_conv_mix, ((0, SUBLANES - CONV_W), (0, 0))),
        cb_mix=b_conv_mix.reshape(1, CONV_CH),
        gdn_aneg=_lane_vec(-jnp.exp(gdn_a_log.astype(F32)), LANE_GDN_A),
        gdn_dtb=_lane_vec(gdn_dt_bias, LANE_GDN_A),
        gdn_norm=gdn_norm.reshape(1, GDN_DV).astype(F32),
        ssd_aneg=_lane_vec(-jnp.exp(ssd_a_log.astype(F32)), LANE_SSD_DT),
        ssd_dtb=_lane_vec(ssd_dt_bias, LANE_SSD_DT),
        ssd_dexp=jnp.repeat(ssd_d.astype(F32), SSD_HEADDIM).reshape(1, SSD_INNER),
        ssd_norm=ssd_norm.reshape(1, SSD_INNER).astype(F32),
        wbg=w_branch_gdn.astype(BF16), wbs=w_branch_ssd.astype(BF16), wo=w_out.astype(BF16),
        w_norm_ffn=w_norm_ffn.reshape(1, D_MODEL),
        wup=w_ffn_up.astype(BF16),
        cw_ffn=jnp.pad(w_ffn_conv, ((0, SUBLANES - FFN_CONV_W), (0, 0))),
        cb_ffn=b_ffn_conv.reshape(1, 2 * D_FF),
        wdn=w_ffn_down.astype(BF16),
        w_norm_final=w_norm_final.reshape(1, D_MODEL),
    )


def _tiles(b, t):
    if t <= CHUNK:
        return dict(nseq=b, tl_in=t, tl_mlp=t, tc=t, cf=256)
    return dict(nseq=1, tl_in=min(t, 1024), tl_mlp=min(t, 512), tc=min(t, 256), cf=256)


def _trunk(x, mod, conv_mix0, s_gdn0, s_ssd0, conv_ffn0, p):
    b, t, _ = x.shape
    tl = _tiles(b, t)
    mod8 = jnp.pad(mod.reshape(b, 6, D_MODEL), ((0, 0), (0, SUBLANES - 6), (0, 0)))
    st_mix = jnp.pad(conv_mix0, ((0, 0), (SUBLANES - (CONV_W - 1), 0), (0, 0)))
    xc, small, cst_mix = _inproj_call(
        x, mod8, p["w_norm_mix"], p["w_conv"], (p["w_small"], p["cw_mix"], p["cb_mix"], st_mix),
        nseq=tl["nseq"], tl=tl["tl_in"])
    raw, = _inproj_call(x, mod8, p["w_norm_mix"], p["w_raw"], None, nseq=tl["nseq"], tl=tl["tl_in"])
    og, s_gdn = _gdn_call(xc, raw, small, s_gdn0, p["gdn_aneg"], p["gdn_dtb"], p["gdn_norm"], tc=tl["tc"])
    h0g = s_ssd0.reshape(b, SSD_GROUPS, SSD_HPG, SSD_STATE, SSD_HEADDIM)
    h0g = jnp.swapaxes(h0g, 2, 3).reshape(b, SSD_GROUPS, SSD_STATE, SSD_GSZ)
    ys, hg = _ssd_call(xc, raw, small, h0g, p["ssd_aneg"], p["ssd_dtb"], p["ssd_dexp"], p["ssd_norm"],
                       tc=tl["tc"])
    s_ssd = jnp.swapaxes(hg.reshape(b, SSD_GROUPS, SSD_STATE, SSD_HPG, SSD_HEADDIM), 2, 3)
    s_ssd = s_ssd.reshape(b, SSD_HEADS, SSD_STATE, SSD_HEADDIM)
    st_ffn = jnp.pad(conv_ffn0, ((0, 0), (SUBLANES - (FFN_CONV_W - 1), 0), (0, 0)))
    y, cst_ffn = _mlp_call(
        x, og, ys, raw, mod8, p["wbg"], p["wbs"], p["wo"], p["w_norm_ffn"], p["wup"],
        p["cw_ffn"], p["cb_ffn"], p["wdn"], p["w_norm_final"], st_ffn,
        nseq=tl["nseq"], tl=tl["tl_mlp"], cf=tl["cf"])
    return (y, cst_mix[:, -1, SUBLANES - (CONV_W - 1):], s_gdn, s_ssd,
            cst_ffn[:, SUBLANES - (FFN_CONV_W - 1):])


def kernel(x_prompt, x_sample, state_conv_mix, state_gdn, state_ssd, state_conv_ffn, c_prompt, c_sample,
           w_ada, b_ada, w_norm_mix, w_in, w_conv_mix, b_conv_mix, gdn_a_log, gdn_dt_bias, gdn_norm,
           ssd_a_log, ssd_dt_bias, ssd_d, ssd_norm, w_branch_gdn, w_branch_ssd, w_out,
           w_norm_ffn, w_ffn_up, w_ffn_conv, b_ffn_conv, w_ffn_down, w_norm_final):
    depth = w_in.shape[0]
    assert depth == 1, "single-layer trunk"
    nb = x_prompt.shape[0]
    ndec = x_sample.shape[0]
    l = 0
    p = _prep_params(w_norm_mix[l], w_in[l], w_conv_mix[l], b_conv_mix[l], gdn_a_log[l], gdn_dt_bias[l],
                     gdn_norm[l], ssd_a_log[l], ssd_dt_bias[l], ssd_d[l], ssd_norm[l], w_branch_gdn[l],
                     w_branch_ssd[l], w_out[l], w_norm_ffn[l], w_ffn_up[l], w_ffn_conv[l], b_ffn_conv[l],
                     w_ffn_down[l], w_norm_final)
    c_all = jnp.concatenate([c_prompt, c_sample], axis=0)
    rows = -(-c_all.shape[0] // SUBLANES) * SUBLANES
    mod = _mod_call(jnp.pad(c_all, ((0, rows - c_all.shape[0]), (0, 0))), w_ada[l], b_ada[l])
    dtype = x_prompt.dtype
    zeros = lambda *shape: jnp.zeros(shape, dtype)
    yp, a1, a2, a3, a4 = _trunk(
        x_prompt, mod[:nb],
        zeros(nb, CONV_W - 1, CONV_CH), zeros(nb, GDN_HEADS, GDN_DK, GDN_DV),
        zeros(nb, SSD_HEADS, SSD_STATE, SSD_HEADDIM), zeros(nb, FFN_CONV_W - 1, 2 * D_FF), p)
    ys, b1, b2, b3, b4 = _trunk(
        x_sample, mod[nb:nb + ndec],
        state_conv_mix[l], state_gdn[l], state_ssd[l], state_conv_ffn[l], p)
    return (yp, ys, a1[None], a2[None], a3[None], a4[None], b1[None], b2[None], b3[None], b4[None])
```

```python
import functools

import jax
import jax.numpy as jnp
from jax import lax
from jax.experimental import pallas as pl
from jax.experimental.pallas import tpu as pltpu

F32 = jnp.float32
BF16 = jnp.bfloat16
HIGHEST = lax.Precision.HIGHEST

D_MODEL = 1024
CHUNK = 64
EPS = 1e-6
CONV_W = 4
FFN_CONV_W = 3
GDN_HEADS = 8
GDN_DK = 128
GDN_DV = 128
GDN_QK = GDN_HEADS * GDN_DK
GDN_V = GDN_HEADS * GDN_DV
SSD_INNER = 2 * D_MODEL
SSD_HEADDIM = 64
SSD_HEADS = SSD_INNER // SSD_HEADDIM
SSD_STATE = 128
SSD_GROUPS = 8
SSD_HPG = SSD_HEADS // SSD_GROUPS
SSD_GSZ = SSD_INNER // SSD_GROUPS
SSD_BC = SSD_GROUPS * SSD_STATE
D_FF = 2816
CONV_CH = 2 * GDN_QK + GDN_V + SSD_INNER + 2 * SSD_BC
RAW_CH = GDN_V + SSD_INNER + 2 * D_MODEL
SMALL_CH = 2 * GDN_HEADS + SSD_HEADS
LANES = 128
SUBLANES = 8
LANE_GDN_B = 0
LANE_GDN_A = GDN_HEADS
LANE_SSD_DT = 2 * GDN_HEADS
VMEM_LIMIT_BYTES = 56 * 1024 * 1024


def _sigmoid(x):
    return 1.0 / (1.0 + jnp.exp(-x))


def _silu(x):
    return x * _sigmoid(x)


def _softplus(x):
    return jnp.maximum(x, 0.0) + jnp.log(1.0 + jnp.exp(-jnp.abs(x)))


def _dot(a, b, precision=None):
    return jnp.dot(a, b, preferred_element_type=F32, precision=precision)


def _dot_nt(a, b):
    return lax.dot_general(a, b, (((1,), (1,)), ((), ())), preferred_element_type=F32)


def _dot_tn(a, b):
    return lax.dot_general(a, b, (((0,), (0,)), ((), ())), preferred_element_type=F32)


def _split_bf16(x):
    hi = x.astype(BF16)
    lo = (x - hi.astype(F32)).astype(BF16)
    return hi, lo


def _const_spec(shape):
    zeros = (0,) * len(shape)
    return pl.BlockSpec(shape, lambda *_: zeros)


def _mod_kernel(c_ref, w_ref, b_ref, o_ref):
    o_ref[...] = _dot(_silu(c_ref[...]), w_ref[...], HIGHEST) + b_ref[...]


def _mod_call(c_pad, w_ada, b_ada):
    rows = c_pad.shape[0]
    n = w_ada.shape[1]
    tn = 512
    return pl.pallas_call(
        _mod_kernel,
        grid=(n // tn,),
        in_specs=[pl.BlockSpec((rows, D_MODEL), lambda j: (0, 0)),
                  pl.BlockSpec((D_MODEL, tn), lambda j: (0, j)),
                  pl.BlockSpec((1, tn), lambda j: (0, j))],
        out_specs=pl.BlockSpec((rows, tn), lambda j: (0, j)),
        out_shape=jax.ShapeDtypeStruct((rows, n), F32),
        name="adaln_mod",
    )(c_pad, w_ada, b_ada.reshape(1, n))


COL_TILE = 1024
COL_SUB = 512
ROW_SUB_CONV = 128
ROW_SUB_RAW = 256


def _inproj_kernel(*refs, nseq, tl, conv, ncs):
    if conv:
        (x_ref, mod_ref, wn_ref, w_ref, ws_ref, cw_ref, cb_ref, st_ref,
         out_ref, sm_ref, cst_ref, h_scr, tail_scr, stage_scr) = refs
    else:
        x_ref, mod_ref, wn_ref, w_ref, out_ref, h_scr = refs
    m = pl.program_id(1)
    n = pl.program_id(2)

    @pl.when(n == 0)
    def _():
        for s in range(nseq):
            x = x_ref[s]
            ms = jnp.mean(x * x, axis=-1, keepdims=True)
            md = mod_ref[s]
            h = (x * lax.rsqrt(ms + EPS) * wn_ref[...]) * (1.0 + md[1:2]) + md[0:1]
            hb = h.astype(BF16)
            h_scr[s * tl:(s + 1) * tl, :] = hb
            if conv:
                sm_ref[s] = _dot(hb, ws_ref[...])

    if conv:
        for s in range(nseq):
            @pl.when(m == 0)
            def _():
                tail_scr[n, s] = st_ref[s]

    sr = min(tl, ROW_SUB_CONV if conv else ROW_SUB_RAW)
    for cj in range(ncs):
        cs = slice(cj * COL_SUB, (cj + 1) * COL_SUB)
        for s in range(nseq):
            for lo in range(0, tl, sr):
                acc = _dot(h_scr[s * tl + lo:s * tl + lo + sr, :], w_ref[cj])
                if not conv:
                    out_ref[s, lo:lo + sr, cs] = acc.astype(out_ref.dtype)
                    continue
                stg = stage_scr.at[s, cj % 2]
                if lo == 0:
                    stg[0:SUBLANES] = tail_scr[n, s, :, cs]
                stg[SUBLANES + lo:SUBLANES + lo + sr] = acc
                ext = stg[lo:lo + sr + SUBLANES]
                y = cb_ref[:, cs] + cw_ref[3:4, cs] * ext[SUBLANES:]
                y = y + cw_ref[2:3, cs] * pltpu.roll(ext, 1, 0)[SUBLANES:]
                y = y + cw_ref[1:2, cs] * pltpu.roll(ext, 2, 0)[SUBLANES:]
                y = y + cw_ref[0:1, cs] * pltpu.roll(ext, 3, 0)[SUBLANES:]
                out_ref[s, lo:lo + sr, cs] = _silu(y).astype(out_ref.dtype)
                if lo + sr == tl:
                    new_tail = stg[tl:tl + SUBLANES]
                    tail_scr[n, s, :, cs] = new_tail
                    cst_ref[s, 0, :, cs] = new_tail


def _inproj_call(x, mod8, w_norm, w, conv_args, *, nseq, tl):
    b, t, _ = x.shape
    tn = w.shape[1] * w.shape[3]
    cols = w.shape[0] * tn
    conv = conv_args is not None
    grid = (b // nseq, t // tl, cols // tn)
    kern = functools.partial(_inproj_kernel, nseq=nseq, tl=tl, conv=conv, ncs=tn // COL_SUB)
    in_specs = [
        pl.BlockSpec((nseq, tl, D_MODEL), lambda i, m, n: (i, m, 0)),
        pl.BlockSpec((nseq, SUBLANES, D_MODEL), lambda i, m, n: (i, 0, 0)),
        pl.BlockSpec((1, D_MODEL), lambda i, m, n: (0, 0)),
        pl.BlockSpec((None, tn // COL_SUB, D_MODEL, COL_SUB), lambda i, m, n: (n, 0, 0, 0)),
    ]
    out_specs = [pl.BlockSpec((None, nseq, tl, tn), lambda i, m, n: (n, i, m, 0))]
    out_shape = [jax.ShapeDtypeStruct((cols // tn, b, t, tn), BF16)]
    scratch = [pltpu.VMEM((nseq * tl, D_MODEL), BF16)]
    args = [x, mod8, w_norm, w]
    if conv:
        in_specs += [
            pl.BlockSpec((D_MODEL, LANES), lambda i, m, n: (0, 0)),
            pl.BlockSpec((SUBLANES, tn), lambda i, m, n: (0, n)),
            pl.BlockSpec((1, tn), lambda i, m, n: (0, n)),
            pl.BlockSpec((nseq, SUBLANES, tn), lambda i, m, n: (i, 0, n)),
        ]
        out_specs += [
            pl.BlockSpec((nseq, tl, LANES), lambda i, m, n: (i, m, 0)),
            pl.BlockSpec((nseq, 1, SUBLANES, tn), lambda i, m, n: (i, m, 0, n)),
        ]
        out_shape += [
            jax.ShapeDtypeStruct((b, t, LANES), F32),
            jax.ShapeDtypeStruct((b, t // tl, SUBLANES, cols), F32),
        ]
        scratch += [
            pltpu.VMEM((cols // tn, nseq, SUBLANES, tn), F32),
            pltpu.VMEM((nseq, 2, tl + SUBLANES, COL_SUB), F32),
        ]
        args += list(conv_args)
    return pl.pallas_call(
        kern,
        grid=grid,
        in_specs=in_specs,
        out_specs=out_specs,
        out_shape=out_shape,
        scratch_shapes=scratch,
        compiler_params=pltpu.CompilerParams(
            dimension_semantics=("arbitrary", "arbitrary", "arbitrary"),
            vmem_limit_bytes=VMEM_LIMIT_BYTES),
        name="inproj_conv" if conv else "inproj_raw",
    )(*args)


INV_BLOCK = 16


def _gdn_stages(q_ref, k_ref, v_ref, z_ref, sm_ref, aneg_ref, dtb_ref, gn_ref, o_ref, s_scr, gsz):
    row_c = lax.broadcasted_iota(jnp.int32, (CHUNK, CHUNK), 0)
    col_c = lax.broadcasted_iota(jnp.int32, (CHUNK, CHUNK), 1)
    tril_b = (row_c >= col_c).astype(BF16)
    qscale = GDN_DK ** -0.5

    heads = range(GDN_HEADS)
    pairs = range(GDN_HEADS // 2)
    hcols = [slice(h * GDN_DK, (h + 1) * GDN_DK) for h in heads]
    row2 = lax.broadcasted_iota(jnp.int32, (CHUNK, 2 * CHUNK), 0)
    col2 = lax.broadcasted_iota(jnp.int32, (CHUNK, 2 * CHUNK), 1) % CHUNK
    first = lax.broadcasted_iota(jnp.int32, (CHUNK, 2 * CHUNK), 1) < CHUNK
    causal2 = row2 >= col2
    strict2 = row2 > col2
    eye2 = (row2 == col2).astype(F32)
    blockdiag2 = (row2 // INV_BLOCK) == (col2 // INV_BLOCK)
    rr = lax.broadcasted_iota(jnp.int32, (2 * CHUNK, 2 * CHUNK), 0)
    cc = lax.broadcasted_iota(jnp.int32, (2 * CHUNK, 2 * CHUNK), 1)
    pair_diag = (rr // CHUNK) == (cc // CHUNK)

    def pair_lhs(a):
        return jnp.concatenate([a[0], a[0], a[1]], axis=1)

    def pair_rhs(b):
        bd = lambda v: jnp.where(pair_diag, jnp.concatenate([v, v], axis=0), 0.0)
        return jnp.concatenate([bd(b[0]), bd(b[1]), bd(b[0])], axis=0)


    def stages(c):
        cis = range(gsz)
        rows = [pl.ds(pl.multiple_of((c * gsz + ci) * CHUNK, CHUNK), CHUNK) for ci in cis]
        beta_all, gam_all, gam_t2, egam_all, elast_all, erem_all = [], [], [], [], [], []
        for ci in cis:
            sm = sm_ref[0, rows[ci], :]
            beta_all.append(_sigmoid(sm))
            g_hi, g_lo = _split_bf16(aneg_ref[...] * _softplus(sm + dtb_ref[...]))
            gam = _dot(tril_b, g_hi) + _dot(tril_b, g_lo)
            gam_all.append(gam)
            gam_t2.append(jnp.concatenate([gam, pltpu.roll(gam, LANES - 1, 1)], axis=0).T)
            egam_all.append(jnp.exp(gam))
            last = gam[CHUNK - 1:CHUNK, :]
            elast_all.append(jnp.exp(last))
            erem_all.append(jnp.exp(last - gam))
        yield
        units = [(ci, h) for ci in cis for h in heads]
        punits = [(ci, p) for ci in cis for p in pairs]
        hcol = lambda arr, lane: arr[:, lane:lane + 1]
        egam_i = {(ci, h): hcol(egam_all[ci], LANE_GDN_A + h) for ci, h in units}
        b_i = {(ci, h): hcol(beta_all[ci], LANE_GDN_B + h) for ci, h in units}
        b_i2, dec2 = {}, {}
        for ci, p in punits:
            lb, lg = LANE_GDN_B + 2 * p, LANE_GDN_A + 2 * p
            b_i2[ci, p] = jnp.where(first, hcol(beta_all[ci], lb), hcol(beta_all[ci], lb + 1))
            gam_i2 = jnp.where(first, hcol(gam_all[ci], lg), hcol(gam_all[ci], lg + 1))
            dec2[ci, p] = jnp.exp(jnp.where(causal2, gam_i2 - gam_t2[ci][lg:lg + 1, :], -jnp.inf))
        yield
        kh, kq = {}, {}
        for ci, h in units:
            qr = q_ref[0, rows[ci], hcols[h]].astype(F32)
            kr = k_ref[0, rows[ci], hcols[h]].astype(F32)
            qn = qr * (lax.rsqrt(jnp.sum(qr * qr, axis=-1, keepdims=True) + EPS) * qscale)
            kn = kr * lax.rsqrt(jnp.sum(kr * kr, axis=-1, keepdims=True) + EPS)
            kh[ci, h] = kn
            kq[ci, h] = jnp.concatenate([kn.astype(BF16), qn.astype(BF16)], axis=0)
        yield
        zk = jnp.zeros((CHUNK, GDN_DK), BF16)
        kqk2 = {}
        for ci, p in punits:
            ka, kb = kq[ci, 2 * p][0:CHUNK], kq[ci, 2 * p + 1][0:CHUNK]
            kbd = jnp.concatenate([jnp.concatenate([ka, zk], axis=1), jnp.concatenate([zk, kb], axis=1)], axis=0)
            kqk2[ci, p] = _dot_nt(jnp.concatenate([kq[ci, 2 * p], kq[ci, 2 * p + 1]], axis=1), kbd)
        yield
        x = {u: jnp.where(strict2, kqk2[u][0:CHUNK] * dec2[u] * (-b_i2[u]), 0.0) for u in punits}
        xd = {u: jnp.where(blockdiag2, x[u], 0.0) for u in punits}
        xn = {u: pair_rhs(_split_bf16(jnp.where(blockdiag2, 0.0, x[u]))) for u in punits}
        yield
        z = {u: _split_bf16(xd[u]) for u in punits}
        pm = {u: eye2 + xd[u] for u in punits}
        for _ in range(3):
            yield
            zf = {u: _dot(pair_lhs(z[u]), pair_rhs(z[u])) for u in punits}
            z = {u: _split_bf16(zf[u]) for u in punits}
            yield
            zr = {u: pair_rhs(z[u]) for u in punits}
            pm = {u: pm[u] + _dot(pair_lhs(_split_bf16(pm[u])), zr[u]) for u in punits}
        yield
        t0 = {u: _split_bf16(pm[u]) for u in punits}
        m = {u: _dot(pair_lhs(t0[u]), xn[u]) for u in punits}
        yield
        ms = {u: _split_bf16(m[u]) for u in punits}
        m2 = {u: pair_rhs(_split_bf16(_dot(pair_lhs(ms[u]), pair_rhs(ms[u])))) for u in punits}
        yield
        ipm = {u: eye2 + m[u] for u in punits}
        qm = {u: ipm[u] + _dot(pair_lhs(_split_bf16(ipm[u])), m2[u]) for u in punits}
        yield
        tinv = {u: pair_lhs(_split_bf16(_dot(pair_lhs(_split_bf16(qm[u])), pair_rhs(t0[u])))) for u in punits}
        yield
        rhs = {(ci, h): _split_bf16(jnp.concatenate(
            [v_ref[0, rows[ci], hcols[h]].astype(F32) * b_i[ci, h], kh[ci, h] * (b_i[ci, h] * egam_i[ci, h])],
            axis=1)) for ci, h in units}
        yield
        zr2 = jnp.zeros((CHUNK, 2 * GDN_DV), BF16)
        sol, qk2 = {}, {}
        for ci, p in punits:
            ra, rb = rhs[ci, 2 * p], rhs[ci, 2 * p + 1]
            bd = lambda i: jnp.concatenate([jnp.concatenate([ra[i], zr2], axis=1),
                                            jnp.concatenate([zr2, rb[i]], axis=1)], axis=0)
            sol2 = _dot(tinv[ci, p], jnp.concatenate([bd(0), bd(1), bd(0)], axis=0))
            sol[ci, 2 * p] = sol2[:, 0:2 * GDN_DV]
            sol[ci, 2 * p + 1] = sol2[:, 2 * GDN_DV:]
            qk2[ci, p] = (kqk2[ci, p][CHUNK:] * dec2[ci, p]).astype(BF16)
        yield
        kdec = {(ci, h): (kh[ci, h] * hcol(erem_all[ci], LANE_GDN_A + h)).astype(BF16) for ci, h in units}
        zu = jnp.zeros((CHUNK, GDN_DV), BF16)
        s = [s_scr[h] for h in heads]
        o = {}
        for ci in cis:
            yield
            sb = [s[h].astype(BF16) for h in heads]
            wq = [jnp.concatenate([sol[ci, h][:, GDN_DV:].astype(BF16), kq[ci, h][CHUNK:]], axis=0) for h in heads]
            ws = [_dot(wq[h], sb[h]) for h in heads]
            yield
            ub = [(sol[ci, h][:, :GDN_DV] - ws[h][0:CHUNK]).astype(BF16) for h in heads]
            for p in pairs:
                ubd = jnp.concatenate([jnp.concatenate([ub[2 * p], zu], axis=1),
                                       jnp.concatenate([zu, ub[2 * p + 1]], axis=1)], axis=0)
                o2 = _dot(qk2[ci, p], ubd)
                o[ci, 2 * p] = egam_i[ci, 2 * p] * ws[2 * p][CHUNK:] + o2[:, :GDN_DV]
                o[ci, 2 * p + 1] = egam_i[ci, 2 * p + 1] * ws[2 * p + 1][CHUNK:] + o2[:, GDN_DV:]
            yield
            s = [hcol(elast_all[ci], LANE_GDN_A + h) * s[h] + _dot_tn(kdec[ci, h], ub[h]) for h in heads]
        yield
        for h in heads:
            s_scr[h] = s[h]
        for ci, h in units:
            zh = z_ref[0, rows[ci], hcols[h]].astype(F32)
            on = (o[ci, h] * lax.rsqrt(jnp.mean(o[ci, h] * o[ci, h], axis=-1, keepdims=True) + EPS)
                  * gn_ref[...] * _silu(zh))
            o_ref[0, rows[ci], hcols[h]] = on.astype(o_ref.dtype)

    return stages


def _group_size(nchunk):
    return 4 if nchunk % 4 == 0 else (2 if nchunk % 2 == 0 else 1)


def _run(*gens):
    live = list(gens)
    while live:
        for g in list(live):
            try:
                next(g)
            except StopIteration:
                live.remove(g)


def _ssd_stages(xlo_ref, xhi_ref, bm_ref, cm_ref, zlo_ref, zhi_ref, sm_ref, aneg_ref, dtb_ref, dexp_ref, nrm_ref,
                y_ref, h_scr):
    pair_w = 2 * SSD_HEADDIM
    row = lax.broadcasted_iota(jnp.int32, (CHUNK, pair_w), 0)
    lane = lax.broadcasted_iota(jnp.int32, (CHUNK, pair_w), 1)
    causal2 = row >= (lane % CHUNK)
    first = lane < SSD_HEADDIM
    rr = lax.broadcasted_iota(jnp.int32, (pair_w, pair_w), 0)
    cc = lax.broadcasted_iota(jnp.int32, (pair_w, pair_w), 1)
    pair_diag = (rr // SSD_HEADDIM) == (cc // SSD_HEADDIM)
    row_c = lax.broadcasted_iota(jnp.int32, (CHUNK, CHUNK), 0)
    col_c = lax.broadcasted_iota(jnp.int32, (CHUNK, CHUNK), 1)
    tril_b = (row_c >= col_c).astype(BF16)
    half = SSD_GROUPS // 2
    groups = range(SSD_GROUPS)
    pairs = range(SSD_HEADS // 2)
    ppg = SSD_HPG // 2

    def stages(c):
        r0 = pl.multiple_of(c * CHUNK, CHUNK)
        rows = pl.ds(r0, CHUNK)
        sm = sm_ref[0, rows, :]
        dt_all = _softplus(sm + dtb_ref[...])
        da_hi, da_lo = _split_bf16(dt_all * aneg_ref[...])
        gam_all = _dot(tril_b, da_hi) + _dot(tril_b, da_lo)
        gam_t2 = jnp.concatenate([gam_all, pltpu.roll(gam_all, LANES - 1, 1)], axis=0).T
        dt_t2 = jnp.concatenate([dt_all, pltpu.roll(dt_all, LANES - 1, 1)], axis=0).T
        yield
        gam_i2, dt_i2 = [], []
        for p in pairs:
            l = LANE_SSD_DT + 2 * p
            gam_i2.append(jnp.where(first, gam_all[:, l:l + 1], gam_all[:, l + 1:l + 2]))
            dt_i2.append(jnp.where(first, dt_all[:, l:l + 1], dt_all[:, l + 1:l + 2]))
        yield
        dec2 = [jnp.exp(jnp.where(causal2, gam_i2[p] - gam_t2[LANE_SSD_DT + 2 * p:LANE_SSD_DT + 2 * p + 1, :],
                                  -jnp.inf)) for p in pairs]
        egam2 = [jnp.exp(gam_i2[p]) for p in pairs]
        last2 = [gam_i2[p][CHUNK - 1:CHUNK, :] for p in pairs]
        wj2 = [jnp.exp(last2[p] - gam_i2[p]) * dt_i2[p] for p in pairs]
        elast2 = [jnp.exp(last2[p]) for p in pairs]

        yield
        xb, xg, bmb, cmb = [], [], [], []
        for g in groups:
            x_ref = xlo_ref if g < half else xhi_ref
            xb.append(x_ref[0, rows, (g % half) * SSD_GSZ:(g % half + 1) * SSD_GSZ])
            xg.append(xb[g].astype(F32))
            bmb.append(bm_ref[0, rows, g * SSD_STATE:(g + 1) * SSD_STATE])
            cmb.append(cm_ref[0, rows, g * SSD_STATE:(g + 1) * SSD_STATE])
        cb2 = [_dot_nt(cmb[g], jnp.concatenate([bmb[g], bmb[g]], axis=0)) for g in groups]
        yield
        ypair = []
        for p in pairs:
            g = p // ppg
            xp = xb[g][:, (p % ppg) * pair_w:(p % ppg + 1) * pair_w]
            xbd = jnp.where(pair_diag, jnp.concatenate([xp, xp], axis=0), 0.0)
            w2 = cb2[g] * dec2[p] * dt_t2[LANE_SSD_DT + 2 * p:LANE_SSD_DT + 2 * p + 1, :]
            ypair.append(_dot(w2.astype(BF16), xbd))
        yield
        hg = [h_scr[g] for g in groups]
        ch = [_dot(cmb[g], hg[g].astype(BF16)) for g in groups]
        yield
        for g in groups:
            ps = range(g * ppg, (g + 1) * ppg)
            wj = jnp.concatenate([wj2[p] for p in ps], axis=1)
            elast = jnp.concatenate([elast2[p] for p in ps], axis=1)
            h_scr[g] = elast * hg[g] + _dot_tn(bmb[g], (xg[g] * wj).astype(BF16))
        yield
        for g in groups:
            ps = range(g * ppg, (g + 1) * ppg)
            z_ref = zlo_ref if g < half else zhi_ref
            ocols = slice(g * SSD_GSZ, (g + 1) * SSD_GSZ)
            egam = jnp.concatenate([egam2[p] for p in ps], axis=1)
            y = jnp.concatenate([ypair[p] for p in ps], axis=1) + ch[g] * egam
            y = y + dexp_ref[:, ocols] * xg[g]
            yz = y * _silu(z_ref[0, rows, (g % half) * SSD_GSZ:(g % half + 1) * SSD_GSZ].astype(F32))
            yn = yz * lax.rsqrt(jnp.mean(yz * yz, axis=-1, keepdims=True) + EPS) * nrm_ref[:, ocols]
            y_ref[0, rows, ocols] = yn.astype(y_ref.dtype)

    return stages


def _scan_kernel(q_ref, k_ref, v_ref, zg_ref, xlo_ref, xhi_ref, bm_ref, cm_ref, zlo_ref, zhi_ref, sm_ref,
                 s0_ref, h0_ref, ganeg_ref, gdtb_ref, gn_ref, saneg_ref, sdtb_ref, dexp_ref, nrm_ref,
                 o_ref, sout_ref, y_ref, hout_ref, s_scr, h_scr, *, nchunk):
    t = pl.program_id(1)

    @pl.when(t == 0)
    def _():
        s_scr[...] = s0_ref[0]
        h_scr[...] = h0_ref[0]

    gsz = _group_size(nchunk)
    gdn = _gdn_stages(q_ref, k_ref, v_ref, zg_ref, sm_ref, ganeg_ref, gdtb_ref, gn_ref, o_ref, s_scr, gsz)
    ssd = _ssd_stages(xlo_ref, xhi_ref, bm_ref, cm_ref, zlo_ref, zhi_ref, sm_ref, saneg_ref, sdtb_ref, dexp_ref,
                      nrm_ref, y_ref, h_scr)

    def ssd_chunks(c):
        for ci in range(gsz):
            yield from ssd(c * gsz + ci)

    def chunk_group(c, carry):
        _run(gdn(c), ssd_chunks(c))
        return carry

    lax.fori_loop(0, nchunk // gsz, chunk_group, 0)

    @pl.when(t == pl.num_programs(1) - 1)
    def _():
        sout_ref[0] = s_scr[...]
        hout_ref[0] = h_scr[...]


def _scan_call(xc, raw, small, s0, h0g, ganeg, gdtb, gnorm, saneg, sdtb, dexp, nrm, *, tc):
    _, b, t, _ = xc.shape
    kern = functools.partial(_scan_kernel, nchunk=tc // CHUNK)
    xct = lambda j: pl.BlockSpec((None, 1, tc, COL_TILE), lambda i, m: (j, i, m, 0))
    gstate = pl.BlockSpec((1, GDN_HEADS, GDN_DK, GDN_DV), lambda i, m: (i, 0, 0, 0))
    sstate = pl.BlockSpec((1, SSD_GROUPS, SSD_STATE, SSD_GSZ), lambda i, m: (i, 0, 0, 0))
    return pl.pallas_call(
        kern,
        grid=(b, t // tc),
        in_specs=[xct(0), xct(1), xct(2), xct(0), xct(3), xct(4), xct(5), xct(6), xct(1), xct(2),
                  pl.BlockSpec((1, tc, LANES), lambda i, m: (i, m, 0)),
                  gstate, sstate,
                  _const_spec((1, LANES)), _const_spec((1, LANES)), _const_spec((1, GDN_DV)),
                  _const_spec((1, LANES)), _const_spec((1, LANES)),
                  _const_spec((1, SSD_INNER)), _const_spec((1, SSD_INNER))],
        out_specs=[pl.BlockSpec((1, tc, GDN_V), lambda i, m: (i, m, 0)), gstate,
                   pl.BlockSpec((1, tc, SSD_INNER), lambda i, m: (i, m, 0)), sstate],
        out_shape=[jax.ShapeDtypeStruct((b, t, GDN_V), BF16),
                   jax.ShapeDtypeStruct((b, GDN_HEADS, GDN_DK, GDN_DV), F32),
                   jax.ShapeDtypeStruct((b, t, SSD_INNER), BF16),
                   jax.ShapeDtypeStruct((b, SSD_GROUPS, SSD_STATE, SSD_GSZ), F32)],
        scratch_shapes=[pltpu.VMEM((GDN_HEADS, GDN_DK, GDN_DV), F32),
                        pltpu.VMEM((SSD_GROUPS, SSD_STATE, SSD_GSZ), F32)],
        compiler_params=pltpu.CompilerParams(
            dimension_semantics=("arbitrary", "arbitrary"),
            vmem_limit_bytes=VMEM_LIMIT_BYTES),
        name="gdn_ssd_scan",
    )(xc, xc, xc, raw, xc, xc, xc, xc, raw, raw, small, s0, h0g, ganeg, gdtb, gnorm, saneg, sdtb, dexp, nrm)


def _mlp_kernel(x_ref, og_ref, ys_ref, ga_ref, gb_ref, mod_ref,
                wbg_ref, wbs_ref, wo_ref, wn2_ref, wup_ref, cw_ref, cb_ref, wdn_ref, wnf_ref, st_ref,
                out_ref, cst_ref, tail_scr, sa_scr, sv_scr, *, nseq, tl, cf):
    m = pl.program_id(1)
    for s in range(nseq):
        @pl.when(m == 0)
        def _():
            tail_scr[s] = st_ref[s]

        md = mod_ref[s]
        mixed = (_sigmoid(ga_ref[s].astype(F32)) * _dot(og_ref[s], wbg_ref[...])
                 + _sigmoid(gb_ref[s].astype(F32)) * _dot(ys_ref[s], wbs_ref[...]))
        x1 = x_ref[s] + md[2:3] * _dot(mixed.astype(BF16), wo_ref[...])
        ms = jnp.mean(x1 * x1, axis=-1, keepdims=True)
        h2 = ((x1 * lax.rsqrt(ms + EPS) * wn2_ref[...]) * (1.0 + md[4:5]) + md[3:4]).astype(BF16)

        def conv(stage, u, cols):
            stage[0:SUBLANES] = tail_scr[s, :, cols]
            stage[SUBLANES:SUBLANES + tl] = u
            y = cb_ref[:, cols] + cw_ref[2:3, cols] * stage[8:8 + tl]
            y = y + cw_ref[1:2, cols] * stage[7:7 + tl]
            y = y + cw_ref[0:1, cols] * stage[6:6 + tl]
            tail_scr[s, :, cols] = stage[tl:tl + SUBLANES]
            return y

        def up(j):
            ca = slice(j * cf, (j + 1) * cf)
            cv = slice(D_FF + j * cf, D_FF + (j + 1) * cf)
            return ca, cv, _dot(h2, wup_ref[:, ca]), _dot(h2, wup_ref[:, cv])

        acc = jnp.zeros((tl, D_MODEL), F32)
        nxt = up(0)
        for j in range(D_FF // cf):
            ca, cv, pa, pv = nxt
            if j + 1 < D_FF // cf:
                nxt = up(j + 1)
            ua = conv(sa_scr.at[j % 2], pa, ca)
            uv = conv(sv_scr.at[j % 2], pv, cv)
            act = (_silu(ua) * uv).astype(BF16)
            acc = acc + _dot(act, wdn_ref[ca, :])
        x2 = x1 + md[5:6] * acc
        ms2 = jnp.mean(x2 * x2, axis=-1, keepdims=True)
        out_ref[s] = x2 * lax.rsqrt(ms2 + EPS) * wnf_ref[...]
        cst_ref[s] = tail_scr[s]


def _mlp_call(x, og, ys, raw, mod8, wbg, wbs, wo, wn2, wup, cw8, cb, wdn, wnf, st8, *, nseq, tl, cf):
    b, t, _ = x.shape
    kern = functools.partial(_mlp_kernel, nseq=nseq, tl=tl, cf=cf)
    tok = lambda width, j: pl.BlockSpec((nseq, tl, width), lambda i, m: (i, m, j))
    raw_tile = lambda j: pl.BlockSpec((None, nseq, tl, COL_TILE), lambda i, m: (j, i, m, 0))
    per_seq = lambda width: pl.BlockSpec((nseq, SUBLANES, width), lambda i, m: (i, 0, 0))

    def resident(shape):
        zeros = (0,) * len(shape)
        return pl.BlockSpec(shape, lambda i, m: zeros, pipeline_mode=pl.Buffered(1))

    return pl.pallas_call(
        kern,
        grid=(b // nseq, t // tl),
        in_specs=[tok(D_MODEL, 0), tok(GDN_V, 0), tok(SSD_INNER, 0), raw_tile(3), raw_tile(4),
                  per_seq(D_MODEL),
                  resident((GDN_V, D_MODEL)), resident((SSD_INNER, D_MODEL)), resident((D_MODEL, D_MODEL)),
                  resident((1, D_MODEL)), resident((D_MODEL, 2 * D_FF)),
                  resident((SUBLANES, 2 * D_FF)), resident((1, 2 * D_FF)),
                  resident((D_FF, D_MODEL)), resident((1, D_MODEL)),
                  per_seq(2 * D_FF)],
        out_specs=[tok(D_MODEL, 0), per_seq(2 * D_FF)],
        out_shape=[jax.ShapeDtypeStruct((b, t, D_MODEL), F32),
                   jax.ShapeDtypeStruct((b, SUBLANES, 2 * D_FF), F32)],
        scratch_shapes=[pltpu.VMEM((nseq, SUBLANES, 2 * D_FF), F32),
                        pltpu.VMEM((2, tl + SUBLANES, cf), F32),
                        pltpu.VMEM((2, tl + SUBLANES, cf), F32)],
        compiler_params=pltpu.CompilerParams(
            dimension_semantics=("arbitrary", "arbitrary"),
            vmem_limit_bytes=VMEM_LIMIT_BYTES),
        name="merge_ffn",
    )(x, og, ys, raw, raw, mod8, wbg, wbs, wo, wn2, wup, cw8, cb, wdn, wnf, st8)


def _lane_vec(values, offset):
    return jnp.zeros((1, LANES), F32).at[0, offset:offset + values.shape[0]].set(values.astype(F32))


def _prep_params(w_norm_mix, w_in, w_conv_mix, b_conv_mix, gdn_a_log, gdn_dt_bias, gdn_norm,
                 ssd_a_log, ssd_dt_bias, ssd_d, ssd_norm, w_branch_gdn, w_branch_ssd, w_out,
                 w_norm_ffn, w_ffn_up, w_ffn_conv, b_ffn_conv, w_ffn_down, w_norm_final):
    o_small = CONV_CH + GDN_V + SSD_INNER
    o_merge = o_small + SMALL_CH
    w_small = jnp.pad(w_in[:, o_small:o_merge], ((0, 0), (0, LANES - SMALL_CH))).astype(BF16)
    w_raw = jnp.concatenate([w_in[:, CONV_CH:o_small], w_in[:, o_merge:]], axis=1)
    col_tiled = lambda w: jnp.transpose(
        w.astype(BF16).reshape(D_MODEL, w.shape[1] // COL_TILE, COL_TILE // COL_SUB, COL_SUB), (1, 2, 0, 3))
    return dict(
        w_norm_mix=w_norm_mix.reshape(1, D_MODEL),
        w_conv=col_tiled(w_in[:, :CONV_CH]),
        w_raw=col_tiled(w_raw),
        w_small=w_small,
        cw_mix=jnp.pad(w_conv_mix, ((0, SUBLANES - CONV_W), (0, 0))),
        cb_mix=b_conv_mix.reshape(1, CONV_CH),
        gdn_aneg=_lane_vec(-jnp.exp(gdn_a_log.astype(F32)), LANE_GDN_A),
        gdn_dtb=_lane_vec(gdn_dt_bias, LANE_GDN_A),
        gdn_norm=gdn_norm.reshape(1, GDN_DV).astype(F32),
        ssd_aneg=_lane_vec(-jnp.exp(ssd_a_log.astype(F32)), LANE_SSD_DT),
        ssd_dtb=_lane_vec(ssd_dt_bias, LANE_SSD_DT),
        ssd_dexp=jnp.repeat(ssd_d.astype(F32), SSD_HEADDIM).reshape(1, SSD_INNER),
        ssd_norm=ssd_norm.reshape(1, SSD_INNER).astype(F32),
        wbg=w_branch_gdn.astype(BF16), wbs=w_branch_ssd.astype(BF16), wo=w_out.astype(BF16),
        w_norm_ffn=w_norm_ffn.reshape(1, D_MODEL),
        wup=w_ffn_up.astype(BF16),
        cw_ffn=jnp.pad(w_ffn_conv, ((0, SUBLANES - FFN_CONV_W), (0, 0))),
        cb_ffn=b_ffn_conv.reshape(1, 2 * D_FF),
        wdn=w_ffn_down.astype(BF16),
        w_norm_final=w_norm_final.reshape(1, D_MODEL),
    )


def _tiles(b, t):
    if t <= CHUNK:
        return dict(nseq=b, tl_in=t, tl_mlp=t, tc=t, cf=256)
    return dict(nseq=1, tl_in=min(t, 1024), tl_mlp=min(t, 512), tc=min(t, 512), cf=256)


def _trunk(x, mod, conv_mix0, s_gdn0, s_ssd0, conv_ffn0, p):
    b, t, _ = x.shape
    tl = _tiles(b, t)
    mod8 = jnp.pad(mod.reshape(b, 6, D_MODEL), ((0, 0), (0, SUBLANES - 6), (0, 0)))
    st_mix = jnp.pad(conv_mix0, ((0, 0), (SUBLANES - (CONV_W - 1), 0), (0, 0)))
    xc, small, cst_mix = _inproj_call(
        x, mod8, p["w_norm_mix"], p["w_conv"], (p["w_small"], p["cw_mix"], p["cb_mix"], st_mix),
        nseq=tl["nseq"], tl=tl["tl_in"])
    raw, = _inproj_call(x, mod8, p["w_norm_mix"], p["w_raw"], None, nseq=tl["nseq"], tl=tl["tl_in"])
    h0g = s_ssd0.reshape(b, SSD_GROUPS, SSD_HPG, SSD_STATE, SSD_HEADDIM)
    h0g = jnp.swapaxes(h0g, 2, 3).reshape(b, SSD_GROUPS, SSD_STATE, SSD_GSZ)
    og, s_gdn, ys, hg = _scan_call(xc, raw, small, s_gdn0, h0g, p["gdn_aneg"], p["gdn_dtb"], p["gdn_norm"],
                                   p["ssd_aneg"], p["ssd_dtb"], p["ssd_dexp"], p["ssd_norm"], tc=tl["tc"])
    s_ssd = jnp.swapaxes(hg.reshape(b, SSD_GROUPS, SSD_STATE, SSD_HPG, SSD_HEADDIM), 2, 3)
    s_ssd = s_ssd.reshape(b, SSD_HEADS, SSD_STATE, SSD_HEADDIM)
    st_ffn = jnp.pad(conv_ffn0, ((0, 0), (SUBLANES - (FFN_CONV_W - 1), 0), (0, 0)))
    y, cst_ffn = _mlp_call(
        x, og, ys, raw, mod8, p["wbg"], p["wbs"], p["wo"], p["w_norm_ffn"], p["wup"],
        p["cw_ffn"], p["cb_ffn"], p["wdn"], p["w_norm_final"], st_ffn,
        nseq=tl["nseq"], tl=tl["tl_mlp"], cf=tl["cf"])
    return (y, cst_mix[:, -1, SUBLANES - (CONV_W - 1):], s_gdn, s_ssd,
            cst_ffn[:, SUBLANES - (FFN_CONV_W - 1):])


def kernel(x_prompt, x_sample, state_conv_mix, state_gdn, state_ssd, state_conv_ffn, c_prompt, c_sample,
           w_ada, b_ada, w_norm_mix, w_in, w_conv_mix, b_conv_mix, gdn_a_log, gdn_dt_bias, gdn_norm,
           ssd_a_log, ssd_dt_bias, ssd_d, ssd_norm, w_branch_gdn, w_branch_ssd, w_out,
           w_norm_ffn, w_ffn_up, w_ffn_conv, b_ffn_conv, w_ffn_down, w_norm_final):
    depth = w_in.shape[0]
    assert depth == 1, "single-layer trunk"
    nb = x_prompt.shape[0]
    ndec = x_sample.shape[0]
    l = 0
    p = _prep_params(w_norm_mix[l], w_in[l], w_conv_mix[l], b_conv_mix[l], gdn_a_log[l], gdn_dt_bias[l],
                     gdn_norm[l], ssd_a_log[l], ssd_dt_bias[l], ssd_d[l], ssd_norm[l], w_branch_gdn[l],
                     w_branch_ssd[l], w_out[l], w_norm_ffn[l], w_ffn_up[l], w_ffn_conv[l], b_ffn_conv[l],
                     w_ffn_down[l], w_norm_final)
    c_all = jnp.concatenate([c_prompt, c_sample], axis=0)
    rows = -(-c_all.shape[0] // SUBLANES) * SUBLANES
    mod = _mod_call(jnp.pad(c_all, ((0, rows - c_all.shape[0]), (0, 0))), w_ada[l], b_ada[l])
    dtype = x_prompt.dtype
    zeros = lambda *shape: jnp.zeros(shape, dtype)
    yp, a1, a2, a3, a4 = _trunk(
        x_prompt, mod[:nb],
        zeros(nb, CONV_W - 1, CONV_CH), zeros(nb, GDN_HEADS, GDN_DK, GDN_DV),
        zeros(nb, SSD_HEADS, SSD_STATE, SSD_HEADDIM), zeros(nb, FFN_CONV_W - 1, 2 * D_FF), p)
    ys, b1, b2, b3, b4 = _trunk(
        x_sample, mod[nb:nb + ndec],
        state_conv_mix[l], state_gdn[l], state_ssd[l], state_conv_ffn[l], p)
    return (yp, ys, a1[None], a2[None], a3[None], a4[None], b1[None], b2[None], b3[None], b4[None])
```

```python
import functools

import jax
import jax.numpy as jnp
from jax import lax
from jax.experimental import pallas as pl
from jax.experimental.pallas import tpu as pltpu

F32 = jnp.float32
BF16 = jnp.bfloat16
HIGHEST = lax.Precision.HIGHEST

D_MODEL = 1024
CHUNK = 64
EPS = 1e-6
CONV_W = 4
FFN_CONV_W = 3
GDN_HEADS = 8
GDN_DK = 128
GDN_DV = 128
GDN_QK = GDN_HEADS * GDN_DK
GDN_V = GDN_HEADS * GDN_DV
SSD_INNER = 2 * D_MODEL
SSD_HEADDIM = 64
SSD_HEADS = SSD_INNER // SSD_HEADDIM
SSD_STATE = 128
SSD_GROUPS = 8
SSD_HPG = SSD_HEADS // SSD_GROUPS
SSD_GSZ = SSD_INNER // SSD_GROUPS
SSD_BC = SSD_GROUPS * SSD_STATE
D_FF = 2816
CONV_CH = 2 * GDN_QK + GDN_V + SSD_INNER + 2 * SSD_BC
RAW_CH = GDN_V + SSD_INNER + 2 * D_MODEL
SMALL_CH = 2 * GDN_HEADS + SSD_HEADS
LANES = 128
SUBLANES = 8
LANE_GDN_B = 0
LANE_GDN_A = GDN_HEADS
LANE_SSD_DT = 2 * GDN_HEADS
VMEM_LIMIT_BYTES = 56 * 1024 * 1024


def _sigmoid(x):
    return 1.0 / (1.0 + jnp.exp(-x))


def _silu(x):
    return x * _sigmoid(x)


def _softplus(x):
    return jnp.maximum(x, 0.0) + jnp.log(1.0 + jnp.exp(-jnp.abs(x)))


def _dot(a, b, precision=None):
    return jnp.dot(a, b, preferred_element_type=F32, precision=precision)


def _dot_nt(a, b):
    return lax.dot_general(a, b, (((1,), (1,)), ((), ())), preferred_element_type=F32)


def _dot_tn(a, b):
    return lax.dot_general(a, b, (((0,), (0,)), ((), ())), preferred_element_type=F32)


def _split_bf16(x):
    hi = x.astype(BF16)
    lo = (x - hi.astype(F32)).astype(BF16)
    return hi, lo


def _const_spec(shape):
    zeros = (0,) * len(shape)
    return pl.BlockSpec(shape, lambda *_: zeros)


def _mod_kernel(c_ref, w_ref, b_ref, o_ref):
    o_ref[...] = _dot(_silu(c_ref[...]), w_ref[...], HIGHEST) + b_ref[...]


def _mod_call(c_pad, w_ada, b_ada):
    rows = c_pad.shape[0]
    n = w_ada.shape[1]
    tn = 512
    return pl.pallas_call(
        _mod_kernel,
        grid=(n // tn,),
        in_specs=[pl.BlockSpec((rows, D_MODEL), lambda j: (0, 0)),
                  pl.BlockSpec((D_MODEL, tn), lambda j: (0, j)),
                  pl.BlockSpec((1, tn), lambda j: (0, j))],
        out_specs=pl.BlockSpec((rows, tn), lambda j: (0, j)),
        out_shape=jax.ShapeDtypeStruct((rows, n), F32),
        name="adaln_mod",
    )(c_pad, w_ada, b_ada.reshape(1, n))


COL_TILE = 1024
COL_SUB = 512
ROW_SUB_CONV = 128
ROW_SUB_RAW = 256


def _inproj_kernel(*refs, nseq, tl, conv, ncs):
    if conv:
        (x_ref, mod_ref, wn_ref, w_ref, ws_ref, cw_ref, cb_ref, st_ref,
         out_ref, sm_ref, cst_ref, h_scr, tail_scr, stage_scr) = refs
    else:
        x_ref, mod_ref, wn_ref, w_ref, out_ref, h_scr = refs
    m = pl.program_id(1)
    n = pl.program_id(2)

    @pl.when(n == 0)
    def _():
        for s in range(nseq):
            x = x_ref[s]
            ms = jnp.mean(x * x, axis=-1, keepdims=True)
            md = mod_ref[s]
            h = (x * lax.rsqrt(ms + EPS) * wn_ref[...]) * (1.0 + md[1:2]) + md[0:1]
            hb = h.astype(BF16)
            h_scr[s * tl:(s + 1) * tl, :] = hb
            if conv:
                sm_ref[s] = _dot(hb, ws_ref[...])

    if conv:
        for s in range(nseq):
            @pl.when(m == 0)
            def _():
                tail_scr[n, s] = st_ref[s]

    sr = min(tl, ROW_SUB_CONV if conv else ROW_SUB_RAW)
    for cj in range(ncs):
        cs = slice(cj * COL_SUB, (cj + 1) * COL_SUB)
        for s in range(nseq):
            for lo in range(0, tl, sr):
                acc = _dot(h_scr[s * tl + lo:s * tl + lo + sr, :], w_ref[cj])
                if not conv:
                    out_ref[s, lo:lo + sr, cs] = acc.astype(out_ref.dtype)
                    continue
                stg = stage_scr.at[s, cj % 2]
                if lo == 0:
                    stg[0:SUBLANES] = tail_scr[n, s, :, cs]
                stg[SUBLANES + lo:SUBLANES + lo + sr] = acc
                ext = stg[lo:lo + sr + SUBLANES]
                y = cb_ref[:, cs] + cw_ref[3:4, cs] * ext[SUBLANES:]
                y = y + cw_ref[2:3, cs] * pltpu.roll(ext, 1, 0)[SUBLANES:]
                y = y + cw_ref[1:2, cs] * pltpu.roll(ext, 2, 0)[SUBLANES:]
                y = y + cw_ref[0:1, cs] * pltpu.roll(ext, 3, 0)[SUBLANES:]
                out_ref[s, lo:lo + sr, cs] = _silu(y).astype(out_ref.dtype)
                if lo + sr == tl:
                    new_tail = stg[tl:tl + SUBLANES]
                    tail_scr[n, s, :, cs] = new_tail
                    cst_ref[s, 0, :, cs] = new_tail


def _inproj_call(x, mod8, w_norm, w, conv_args, *, nseq, tl):
    b, t, _ = x.shape
    tn = w.shape[1] * w.shape[3]
    cols = w.shape[0] * tn
    conv = conv_args is not None
    grid = (b // nseq, t // tl, cols // tn)
    kern = functools.partial(_inproj_kernel, nseq=nseq, tl=tl, conv=conv, ncs=tn // COL_SUB)
    in_specs = [
        pl.BlockSpec((nseq, tl, D_MODEL), lambda i, m, n: (i, m, 0)),
        pl.BlockSpec((nseq, SUBLANES, D_MODEL), lambda i, m, n: (i, 0, 0)),
        pl.BlockSpec((1, D_MODEL), lambda i, m, n: (0, 0)),
        pl.BlockSpec((None, tn // COL_SUB, D_MODEL, COL_SUB), lambda i, m, n: (n, 0, 0, 0)),
    ]
    out_specs = [pl.BlockSpec((None, nseq, tl, tn), lambda i, m, n: (n, i, m, 0))]
    out_shape = [jax.ShapeDtypeStruct((cols // tn, b, t, tn), BF16)]
    scratch = [pltpu.VMEM((nseq * tl, D_MODEL), BF16)]
    args = [x, mod8, w_norm, w]
    if conv:
        in_specs += [
            pl.BlockSpec((D_MODEL, LANES), lambda i, m, n: (0, 0)),
            pl.BlockSpec((SUBLANES, tn), lambda i, m, n: (0, n)),
            pl.BlockSpec((1, tn), lambda i, m, n: (0, n)),
            pl.BlockSpec((nseq, SUBLANES, tn), lambda i, m, n: (i, 0, n)),
        ]
        out_specs += [
            pl.BlockSpec((nseq, tl, LANES), lambda i, m, n: (i, m, 0)),
            pl.BlockSpec((nseq, 1, SUBLANES, tn), lambda i, m, n: (i, m, 0, n)),
        ]
        out_shape += [
            jax.ShapeDtypeStruct((b, t, LANES), F32),
            jax.ShapeDtypeStruct((b, t // tl, SUBLANES, cols), F32),
        ]
        scratch += [
            pltpu.VMEM((cols // tn, nseq, SUBLANES, tn), F32),
            pltpu.VMEM((nseq, 2, tl + SUBLANES, COL_SUB), F32),
        ]
        args += list(conv_args)
    return pl.pallas_call(
        kern,
        grid=grid,
        in_specs=in_specs,
        out_specs=out_specs,
        out_shape=out_shape,
        scratch_shapes=scratch,
        compiler_params=pltpu.CompilerParams(
            dimension_semantics=("arbitrary", "arbitrary", "arbitrary"),
            vmem_limit_bytes=VMEM_LIMIT_BYTES),
        name="inproj_conv" if conv else "inproj_raw",
    )(*args)


INV_BLOCK = 16


def _gdn_stages(q_ref, k_ref, v_ref, z_ref, sm_ref, aneg_ref, dtb_ref, gn_ref, o_ref, s_scr, gsz):
    row_c = lax.broadcasted_iota(jnp.int32, (CHUNK, CHUNK), 0)
    col_c = lax.broadcasted_iota(jnp.int32, (CHUNK, CHUNK), 1)
    tril_b = (row_c >= col_c).astype(BF16)
    qscale = GDN_DK ** -0.5

    heads = range(GDN_HEADS)
    pairs = range(GDN_HEADS // 2)
    hcols = [slice(h * GDN_DK, (h + 1) * GDN_DK) for h in heads]
    row2 = lax.broadcasted_iota(jnp.int32, (CHUNK, 2 * CHUNK), 0)
    col2 = lax.broadcasted_iota(jnp.int32, (CHUNK, 2 * CHUNK), 1) % CHUNK
    first = lax.broadcasted_iota(jnp.int32, (CHUNK, 2 * CHUNK), 1) < CHUNK
    causal2 = row2 >= col2
    strict2 = row2 > col2
    eye2 = (row2 == col2).astype(F32)
    blockdiag2 = (row2 // INV_BLOCK) == (col2 // INV_BLOCK)
    rr = lax.broadcasted_iota(jnp.int32, (2 * CHUNK, 2 * CHUNK), 0)
    cc = lax.broadcasted_iota(jnp.int32, (2 * CHUNK, 2 * CHUNK), 1)
    pair_diag = (rr // CHUNK) == (cc // CHUNK)

    def pair_lhs(a):
        return jnp.concatenate([a[0], a[0], a[1]], axis=1)

    def pair_rhs(b):
        bd = lambda v: jnp.where(pair_diag, jnp.concatenate([v, v], axis=0), 0.0)
        return jnp.concatenate([bd(b[0]), bd(b[1]), bd(b[0])], axis=0)


    def stages(c):
        cis = range(gsz)
        rows = [pl.ds(pl.multiple_of((c * gsz + ci) * CHUNK, CHUNK), CHUNK) for ci in cis]
        beta_all, gam_all, gam_t2, egam_all, elast_all, erem_all = [], [], [], [], [], []
        for ci in cis:
            sm = sm_ref[0, rows[ci], :]
            beta_all.append(_sigmoid(sm))
            g_hi, g_lo = _split_bf16(aneg_ref[...] * _softplus(sm + dtb_ref[...]))
            gam = _dot(tril_b, g_hi) + _dot(tril_b, g_lo)
            gam_all.append(gam)
            gam_t2.append(jnp.concatenate([gam, pltpu.roll(gam, LANES - 1, 1)], axis=0).T)
            egam_all.append(jnp.exp(gam))
            last = gam[CHUNK - 1:CHUNK, :]
            elast_all.append(jnp.exp(last))
            erem_all.append(jnp.exp(last - gam))
        yield
        units = [(ci, h) for ci in cis for h in heads]
        punits = [(ci, p) for ci in cis for p in pairs]
        hcol = lambda arr, lane: arr[:, lane:lane + 1]
        egam_i = {(ci, h): hcol(egam_all[ci], LANE_GDN_A + h) for ci, h in units}
        b_i = {(ci, h): hcol(beta_all[ci], LANE_GDN_B + h) for ci, h in units}
        b_i2, dec2 = {}, {}
        for ci, p in punits:
            lb, lg = LANE_GDN_B + 2 * p, LANE_GDN_A + 2 * p
            b_i2[ci, p] = jnp.where(first, hcol(beta_all[ci], lb), hcol(beta_all[ci], lb + 1))
            gam_i2 = jnp.where(first, hcol(gam_all[ci], lg), hcol(gam_all[ci], lg + 1))
            dec2[ci, p] = jnp.exp(jnp.where(causal2, gam_i2 - gam_t2[ci][lg:lg + 1, :], -jnp.inf))
        yield
        kh, kq = {}, {}
        for ci, h in units:
            qr = q_ref[0, rows[ci], hcols[h]].astype(F32)
            kr = k_ref[0, rows[ci], hcols[h]].astype(F32)
            qn = qr * (lax.rsqrt(jnp.sum(qr * qr, axis=-1, keepdims=True) + EPS) * qscale)
            kn = kr * lax.rsqrt(jnp.sum(kr * kr, axis=-1, keepdims=True) + EPS)
            kh[ci, h] = kn
            kq[ci, h] = jnp.concatenate([kn.astype(BF16), qn.astype(BF16)], axis=0)
        yield
        zk = jnp.zeros((CHUNK, GDN_DK), BF16)
        kqk2 = {}
        for ci, p in punits:
            ka, kb = kq[ci, 2 * p][0:CHUNK], kq[ci, 2 * p + 1][0:CHUNK]
            kbd = jnp.concatenate([jnp.concatenate([ka, zk], axis=1), jnp.concatenate([zk, kb], axis=1)], axis=0)
            kqk2[ci, p] = _dot_nt(jnp.concatenate([kq[ci, 2 * p], kq[ci, 2 * p + 1]], axis=1), kbd)
        yield
        x = {u: jnp.where(strict2, kqk2[u][0:CHUNK] * dec2[u] * (-b_i2[u]), 0.0) for u in punits}
        xd = {u: jnp.where(blockdiag2, x[u], 0.0) for u in punits}
        xn = {u: pair_rhs(_split_bf16(jnp.where(blockdiag2, 0.0, x[u]))) for u in punits}
        yield
        z = {u: _split_bf16(xd[u]) for u in punits}
        pm = {u: eye2 + xd[u] for u in punits}
        for _ in range(3):
            yield
            zf = {u: _dot(pair_lhs(z[u]), pair_rhs(z[u])) for u in punits}
            z = {u: _split_bf16(zf[u]) for u in punits}
            yield
            zr = {u: pair_rhs(z[u]) for u in punits}
            pm = {u: pm[u] + _dot(pair_lhs(_split_bf16(pm[u])), zr[u]) for u in punits}
        yield
        t0 = {u: _split_bf16(pm[u]) for u in punits}
        m = {u: _dot(pair_lhs(t0[u]), xn[u]) for u in punits}
        yield
        ms = {u: _split_bf16(m[u]) for u in punits}
        m2 = {u: pair_rhs(_split_bf16(_dot(pair_lhs(ms[u]), pair_rhs(ms[u])))) for u in punits}
        yield
        ipm = {u: eye2 + m[u] for u in punits}
        qm = {u: ipm[u] + _dot(pair_lhs(_split_bf16(ipm[u])), m2[u]) for u in punits}
        yield
        tinv = {u: pair_lhs(_split_bf16(_dot(pair_lhs(_split_bf16(qm[u])), pair_rhs(t0[u])))) for u in punits}
        yield
        rhs = {(ci, h): _split_bf16(jnp.concatenate(
            [v_ref[0, rows[ci], hcols[h]].astype(F32) * b_i[ci, h], kh[ci, h] * (b_i[ci, h] * egam_i[ci, h])],
            axis=1)) for ci, h in units}
        yield
        zr2 = jnp.zeros((CHUNK, 2 * GDN_DV), BF16)
        sol, qk2 = {}, {}
        for ci, p in punits:
            ra, rb = rhs[ci, 2 * p], rhs[ci, 2 * p + 1]
            bd = lambda i: jnp.concatenate([jnp.concatenate([ra[i], zr2], axis=1),
                                            jnp.concatenate([zr2, rb[i]], axis=1)], axis=0)
            sol2 = _dot(tinv[ci, p], jnp.concatenate([bd(0), bd(1), bd(0)], axis=0))
            sol[ci, 2 * p] = sol2[:, 0:2 * GDN_DV]
            sol[ci, 2 * p + 1] = sol2[:, 2 * GDN_DV:]
            qk2[ci, p] = (kqk2[ci, p][CHUNK:] * dec2[ci, p]).astype(BF16)
        yield
        kdec = {(ci, h): (kh[ci, h] * hcol(erem_all[ci], LANE_GDN_A + h)).astype(BF16) for ci, h in units}
        zu = jnp.zeros((CHUNK, GDN_DV), BF16)
        s = [s_scr[h] for h in heads]
        o = {}
        for ci in cis:
            yield
            sb = [s[h].astype(BF16) for h in heads]
            wq = [jnp.concatenate([sol[ci, h][:, GDN_DV:].astype(BF16), kq[ci, h][CHUNK:]], axis=0) for h in heads]
            ws = [_dot(wq[h], sb[h]) for h in heads]
            yield
            ub = [(sol[ci, h][:, :GDN_DV] - ws[h][0:CHUNK]).astype(BF16) for h in heads]
            for p in pairs:
                ubd = jnp.concatenate([jnp.concatenate([ub[2 * p], zu], axis=1),
                                       jnp.concatenate([zu, ub[2 * p + 1]], axis=1)], axis=0)
                o2 = _dot(qk2[ci, p], ubd)
                o[ci, 2 * p] = egam_i[ci, 2 * p] * ws[2 * p][CHUNK:] + o2[:, :GDN_DV]
                o[ci, 2 * p + 1] = egam_i[ci, 2 * p + 1] * ws[2 * p + 1][CHUNK:] + o2[:, GDN_DV:]
            yield
            s = [hcol(elast_all[ci], LANE_GDN_A + h) * s[h] + _dot_tn(kdec[ci, h], ub[h]) for h in heads]
        yield
        for h in heads:
            s_scr[h] = s[h]
        for ci, h in units:
            zh = z_ref[0, rows[ci], hcols[h]].astype(F32)
            on = (o[ci, h] * lax.rsqrt(jnp.mean(o[ci, h] * o[ci, h], axis=-1, keepdims=True) + EPS)
                  * gn_ref[...] * _silu(zh))
            o_ref[0, rows[ci], hcols[h]] = on.astype(o_ref.dtype)

    return stages


def _group_size(nchunk):
    return 4 if nchunk % 4 == 0 else (2 if nchunk % 2 == 0 else 1)


def _run(*gens):
    live = list(gens)
    while live:
        for g in list(live):
            try:
                next(g)
            except StopIteration:
                live.remove(g)


def _ssd_stages(xlo_ref, xhi_ref, bm_ref, cm_ref, zlo_ref, zhi_ref, sm_ref, aneg_ref, dtb_ref, dexp_ref, nrm_ref,
                y_ref, h_scr):
    pair_w = 2 * SSD_HEADDIM
    row = lax.broadcasted_iota(jnp.int32, (CHUNK, pair_w), 0)
    lane = lax.broadcasted_iota(jnp.int32, (CHUNK, pair_w), 1)
    causal2 = row >= (lane % CHUNK)
    first = lane < SSD_HEADDIM
    rr = lax.broadcasted_iota(jnp.int32, (pair_w, pair_w), 0)
    cc = lax.broadcasted_iota(jnp.int32, (pair_w, pair_w), 1)
    pair_diag = (rr // SSD_HEADDIM) == (cc // SSD_HEADDIM)
    row_c = lax.broadcasted_iota(jnp.int32, (CHUNK, CHUNK), 0)
    col_c = lax.broadcasted_iota(jnp.int32, (CHUNK, CHUNK), 1)
    tril_b = (row_c >= col_c).astype(BF16)
    half = SSD_GROUPS // 2
    groups = range(SSD_GROUPS)
    pairs = range(SSD_HEADS // 2)
    ppg = SSD_HPG // 2

    def stages(c):
        r0 = pl.multiple_of(c * CHUNK, CHUNK)
        rows = pl.ds(r0, CHUNK)
        sm = sm_ref[0, rows, :]
        dt_all = _softplus(sm + dtb_ref[...])
        da_hi, da_lo = _split_bf16(dt_all * aneg_ref[...])
        gam_all = _dot(tril_b, da_hi) + _dot(tril_b, da_lo)
        gam_t2 = jnp.concatenate([gam_all, pltpu.roll(gam_all, LANES - 1, 1)], axis=0).T
        dt_t2 = jnp.concatenate([dt_all, pltpu.roll(dt_all, LANES - 1, 1)], axis=0).T
        yield
        gam_i2, dt_i2 = [], []
        for p in pairs:
            l = LANE_SSD_DT + 2 * p
            gam_i2.append(jnp.where(first, gam_all[:, l:l + 1], gam_all[:, l + 1:l + 2]))
            dt_i2.append(jnp.where(first, dt_all[:, l:l + 1], dt_all[:, l + 1:l + 2]))
        yield
        dec2 = [jnp.exp(jnp.where(causal2, gam_i2[p] - gam_t2[LANE_SSD_DT + 2 * p:LANE_SSD_DT + 2 * p + 1, :],
                                  -jnp.inf)) for p in pairs]
        egam2 = [jnp.exp(gam_i2[p]) for p in pairs]
        last2 = [gam_i2[p][CHUNK - 1:CHUNK, :] for p in pairs]
        wj2 = [jnp.exp(last2[p] - gam_i2[p]) * dt_i2[p] for p in pairs]
        elast2 = [jnp.exp(last2[p]) for p in pairs]

        yield
        xb, xg, bmb, cmb = [], [], [], []
        for g in groups:
            x_ref = xlo_ref if g < half else xhi_ref
            xb.append(x_ref[0, rows, (g % half) * SSD_GSZ:(g % half + 1) * SSD_GSZ])
            xg.append(xb[g].astype(F32))
            bmb.append(bm_ref[0, rows, g * SSD_STATE:(g + 1) * SSD_STATE])
            cmb.append(cm_ref[0, rows, g * SSD_STATE:(g + 1) * SSD_STATE])
        cb2 = [_dot_nt(cmb[g], jnp.concatenate([bmb[g], bmb[g]], axis=0)) for g in groups]
        yield
        ypair = []
        for p in pairs:
            g = p // ppg
            xp = xb[g][:, (p % ppg) * pair_w:(p % ppg + 1) * pair_w]
            xbd = jnp.where(pair_diag, jnp.concatenate([xp, xp], axis=0), 0.0)
            w2 = cb2[g] * dec2[p] * dt_t2[LANE_SSD_DT + 2 * p:LANE_SSD_DT + 2 * p + 1, :]
            ypair.append(_dot(w2.astype(BF16), xbd))
        yield
        hg = [h_scr[g] for g in groups]
        ch = [_dot(cmb[g], hg[g].astype(BF16)) for g in groups]
        yield
        for g in groups:
            ps = range(g * ppg, (g + 1) * ppg)
            wj = jnp.concatenate([wj2[p] for p in ps], axis=1)
            elast = jnp.concatenate([elast2[p] for p in ps], axis=1)
            h_scr[g] = elast * hg[g] + _dot_tn(bmb[g], (xg[g] * wj).astype(BF16))
        yield
        for g in groups:
            ps = range(g * ppg, (g + 1) * ppg)
            z_ref = zlo_ref if g < half else zhi_ref
            ocols = slice(g * SSD_GSZ, (g + 1) * SSD_GSZ)
            egam = jnp.concatenate([egam2[p] for p in ps], axis=1)
            y = jnp.concatenate([ypair[p] for p in ps], axis=1) + ch[g] * egam
            y = y + dexp_ref[:, ocols] * xg[g]
            yz = y * _silu(z_ref[0, rows, (g % half) * SSD_GSZ:(g % half + 1) * SSD_GSZ].astype(F32))
            yn = yz * lax.rsqrt(jnp.mean(yz * yz, axis=-1, keepdims=True) + EPS) * nrm_ref[:, ocols]
            y_ref[0, rows, ocols] = yn.astype(y_ref.dtype)

    return stages


def _scan_kernel(q_ref, k_ref, v_ref, zg_ref, xlo_ref, xhi_ref, bm_ref, cm_ref, zlo_ref, zhi_ref, sm_ref,
                 s0_ref, h0_ref, ganeg_ref, gdtb_ref, gn_ref, saneg_ref, sdtb_ref, dexp_ref, nrm_ref,
                 o_ref, sout_ref, y_ref, hout_ref, s_scr, h_scr, *, nchunk):
    t = pl.program_id(1)

    @pl.when(t == 0)
    def _():
        s_scr[...] = s0_ref[0]
        h_scr[...] = h0_ref[0]

    gsz = _group_size(nchunk)
    gdn = _gdn_stages(q_ref, k_ref, v_ref, zg_ref, sm_ref, ganeg_ref, gdtb_ref, gn_ref, o_ref, s_scr, gsz)
    ssd = _ssd_stages(xlo_ref, xhi_ref, bm_ref, cm_ref, zlo_ref, zhi_ref, sm_ref, saneg_ref, sdtb_ref, dexp_ref,
                      nrm_ref, y_ref, h_scr)

    def ssd_chunks(c):
        for ci in range(gsz):
            yield from ssd(c * gsz + ci)

    def chunk_group(c, carry):
        _run(gdn(c), ssd_chunks(c))
        return carry

    lax.fori_loop(0, nchunk // gsz, chunk_group, 0)

    @pl.when(t == pl.num_programs(1) - 1)
    def _():
        sout_ref[0] = s_scr[...]
        hout_ref[0] = h_scr[...]


def _scan_call(xc, raw, small, s0, h0g, ganeg, gdtb, gnorm, saneg, sdtb, dexp, nrm, *, tc):
    _, b, t, _ = xc.shape
    kern = functools.partial(_scan_kernel, nchunk=tc // CHUNK)
    xct = lambda j: pl.BlockSpec((None, 1, tc, COL_TILE), lambda i, m: (j, i, m, 0))
    gstate = pl.BlockSpec((1, GDN_HEADS, GDN_DK, GDN_DV), lambda i, m: (i, 0, 0, 0))
    sstate = pl.BlockSpec((1, SSD_GROUPS, SSD_STATE, SSD_GSZ), lambda i, m: (i, 0, 0, 0))
    return pl.pallas_call(
        kern,
        grid=(b, t // tc),
        in_specs=[xct(0), xct(1), xct(2), xct(0), xct(3), xct(4), xct(5), xct(6), xct(1), xct(2),
                  pl.BlockSpec((1, tc, LANES), lambda i, m: (i, m, 0)),
                  gstate, sstate,
                  _const_spec((1, LANES)), _const_spec((1, LANES)), _const_spec((1, GDN_DV)),
                  _const_spec((1, LANES)), _const_spec((1, LANES)),
                  _const_spec((1, SSD_INNER)), _const_spec((1, SSD_INNER))],
        out_specs=[pl.BlockSpec((1, tc, GDN_V), lambda i, m: (i, m, 0)), gstate,
                   pl.BlockSpec((1, tc, SSD_INNER), lambda i, m: (i, m, 0)), sstate],
        out_shape=[jax.ShapeDtypeStruct((b, t, GDN_V), BF16),
                   jax.ShapeDtypeStruct((b, GDN_HEADS, GDN_DK, GDN_DV), F32),
                   jax.ShapeDtypeStruct((b, t, SSD_INNER), BF16),
                   jax.ShapeDtypeStruct((b, SSD_GROUPS, SSD_STATE, SSD_GSZ), F32)],
        scratch_shapes=[pltpu.VMEM((GDN_HEADS, GDN_DK, GDN_DV), F32),
                        pltpu.VMEM((SSD_GROUPS, SSD_STATE, SSD_GSZ), F32)],
        compiler_params=pltpu.CompilerParams(
            dimension_semantics=("arbitrary", "arbitrary"),
            vmem_limit_bytes=VMEM_LIMIT_BYTES),
        name="gdn_ssd_scan",
    )(xc, xc, xc, raw, xc, xc, xc, xc, raw, raw, small, s0, h0g, ganeg, gdtb, gnorm, saneg, sdtb, dexp, nrm)


def _mlp_kernel(x_ref, og_ref, ys_ref, ga_ref, gb_ref, mod_ref,
                wbg_ref, wbs_ref, wo_ref, wn2_ref, wup_ref, cw_ref, cb_ref, wdn_ref, wnf_ref, st_ref,
                out_ref, cst_ref, tail_scr, sa_scr, sv_scr, *, nseq, tl, cf):
    m = pl.program_id(1)
    for s in range(nseq):
        @pl.when(m == 0)
        def _():
            tail_scr[s] = st_ref[s]

        md = mod_ref[s]
        mixed = (_sigmoid(ga_ref[s].astype(F32)) * _dot(og_ref[s], wbg_ref[...])
                 + _sigmoid(gb_ref[s].astype(F32)) * _dot(ys_ref[s], wbs_ref[...]))
        x1 = x_ref[s] + md[2:3] * _dot(mixed.astype(BF16), wo_ref[...])
        ms = jnp.mean(x1 * x1, axis=-1, keepdims=True)
        h2 = ((x1 * lax.rsqrt(ms + EPS) * wn2_ref[...]) * (1.0 + md[4:5]) + md[3:4]).astype(BF16)

        def conv(stage, u, cols):
            stage[0:SUBLANES] = tail_scr[s, :, cols]
            stage[SUBLANES:SUBLANES + tl] = u
            y = cb_ref[:, cols] + cw_ref[2:3, cols] * stage[8:8 + tl]
            y = y + cw_ref[1:2, cols] * stage[7:7 + tl]
            y = y + cw_ref[0:1, cols] * stage[6:6 + tl]
            tail_scr[s, :, cols] = stage[tl:tl + SUBLANES]
            return y

        def up(j):
            ca = slice(j * cf, (j + 1) * cf)
            cv = slice(D_FF + j * cf, D_FF + (j + 1) * cf)
            return ca, cv, _dot(h2, wup_ref[:, ca]), _dot(h2, wup_ref[:, cv])

        acc = jnp.zeros((tl, D_MODEL), F32)
        nxt = up(0)
        for j in range(D_FF // cf):
            ca, cv, pa, pv = nxt
            if j + 1 < D_FF // cf:
                nxt = up(j + 1)
            ua = conv(sa_scr.at[j % 2], pa, ca)
            uv = conv(sv_scr.at[j % 2], pv, cv)
            act = (_silu(ua) * uv).astype(BF16)
            acc = acc + _dot(act, wdn_ref[ca, :])
        x2 = x1 + md[5:6] * acc
        ms2 = jnp.mean(x2 * x2, axis=-1, keepdims=True)
        out_ref[s] = x2 * lax.rsqrt(ms2 + EPS) * wnf_ref[...]
        cst_ref[s] = tail_scr[s]


def _mlp_call(x, og, ys, raw, mod8, wbg, wbs, wo, wn2, wup, cw8, cb, wdn, wnf, st8, *, nseq, tl, cf):
    b, t, _ = x.shape
    kern = functools.partial(_mlp_kernel, nseq=nseq, tl=tl, cf=cf)
    tok = lambda width, j: pl.BlockSpec((nseq, tl, width), lambda i, m: (i, m, j))
    raw_tile = lambda j: pl.BlockSpec((None, nseq, tl, COL_TILE), lambda i, m: (j, i, m, 0))
    per_seq = lambda width: pl.BlockSpec((nseq, SUBLANES, width), lambda i, m: (i, 0, 0))

    def resident(shape):
        zeros = (0,) * len(shape)
        return pl.BlockSpec(shape, lambda i, m: zeros, pipeline_mode=pl.Buffered(1))

    return pl.pallas_call(
        kern,
        grid=(b // nseq, t // tl),
        in_specs=[tok(D_MODEL, 0), tok(GDN_V, 0), tok(SSD_INNER, 0), raw_tile(3), raw_tile(4),
                  per_seq(D_MODEL),
                  resident((GDN_V, D_MODEL)), resident((SSD_INNER, D_MODEL)), resident((D_MODEL, D_MODEL)),
                  resident((1, D_MODEL)), resident((D_MODEL, 2 * D_FF)),
                  resident((SUBLANES, 2 * D_FF)), resident((1, 2 * D_FF)),
                  resident((D_FF, D_MODEL)), resident((1, D_MODEL)),
                  per_seq(2 * D_FF)],
        out_specs=[tok(D_MODEL, 0), per_seq(2 * D_FF)],
        out_shape=[jax.ShapeDtypeStruct((b, t, D_MODEL), F32),
                   jax.ShapeDtypeStruct((b, SUBLANES, 2 * D_FF), F32)],
        scratch_shapes=[pltpu.VMEM((nseq, SUBLANES, 2 * D_FF), F32),
                        pltpu.VMEM((2, tl + SUBLANES, cf), F32),
                        pltpu.VMEM((2, tl + SUBLANES, cf), F32)],
        compiler_params=pltpu.CompilerParams(
            dimension_semantics=("arbitrary", "arbitrary"),
            vmem_limit_bytes=VMEM_LIMIT_BYTES),
        name="merge_ffn",
    )(x, og, ys, raw, raw, mod8, wbg, wbs, wo, wn2, wup, cw8, cb, wdn, wnf, st8)


def _lane_vec(values, offset):
    return jnp.zeros((1, LANES), F32).at[0, offset:offset + values.shape[0]].set(values.astype(F32))


def _prep_params(w_norm_mix, w_in, w_conv_mix, b_conv_mix, gdn_a_log, gdn_dt_bias, gdn_norm,
                 ssd_a_log, ssd_dt_bias, ssd_d, ssd_norm, w_branch_gdn, w_branch_ssd, w_out,
                 w_norm_ffn, w_ffn_up, w_ffn_conv, b_ffn_conv, w_ffn_down, w_norm_final):
    o_small = CONV_CH + GDN_V + SSD_INNER
    o_merge = o_small + SMALL_CH
    w_bf = w_in.astype(BF16)
    w_small = jnp.pad(w_bf[:, o_small:o_merge], ((0, 0), (0, LANES - SMALL_CH)))
    w_raw = jnp.concatenate([w_bf[:, CONV_CH:o_small], w_bf[:, o_merge:]], axis=1)
    col_tiled = lambda w: jnp.transpose(
        w.reshape(D_MODEL, w.shape[1] // COL_TILE, COL_TILE // COL_SUB, COL_SUB), (1, 2, 0, 3))
    return dict(
        w_norm_mix=w_norm_mix.reshape(1, D_MODEL),
        w_conv=col_tiled(w_bf[:, :CONV_CH]),
        w_raw=col_tiled(w_raw),
        w_small=w_small,
        cw_mix=jnp.pad(w_conv_mix, ((0, SUBLANES - CONV_W), (0, 0))),
        cb_mix=b_conv_mix.reshape(1, CONV_CH),
        gdn_aneg=_lane_vec(-jnp.exp(gdn_a_log.astype(F32)), LANE_GDN_A),
        gdn_dtb=_lane_vec(gdn_dt_bias, LANE_GDN_A),
        gdn_norm=gdn_norm.reshape(1, GDN_DV).astype(F32),
        ssd_aneg=_lane_vec(-jnp.exp(ssd_a_log.astype(F32)), LANE_SSD_DT),
        ssd_dtb=_lane_vec(ssd_dt_bias, LANE_SSD_DT),
        ssd_dexp=jnp.repeat(ssd_d.astype(F32), SSD_HEADDIM).reshape(1, SSD_INNER),
        ssd_norm=ssd_norm.reshape(1, SSD_INNER).astype(F32),
        wbg=w_branch_gdn.astype(BF16), wbs=w_branch_ssd.astype(BF16), wo=w_out.astype(BF16),
        w_norm_ffn=w_norm_ffn.reshape(1, D_MODEL),
        wup=w_ffn_up.astype(BF16),
        cw_ffn=jnp.pad(w_ffn_conv, ((0, SUBLANES - FFN_CONV_W), (0, 0))),
        cb_ffn=b_ffn_conv.reshape(1, 2 * D_FF),
        wdn=w_ffn_down.astype(BF16),
        w_norm_final=w_norm_final.reshape(1, D_MODEL),
    )


def _tiles(b, t):
    if t <= CHUNK:
        return dict(nseq=b, tl_in=t, tl_mlp=t, tc=t, cf=256)
    return dict(nseq=1, tl_in=min(t, 1024), tl_mlp=min(t, 512), tc=min(t, 512), cf=256)


def _trunk(x, mod, conv_mix0, s_gdn0, s_ssd0, conv_ffn0, p):
    b, t, _ = x.shape
    tl = _tiles(b, t)
    mod8 = jnp.pad(mod.reshape(b, 6, D_MODEL), ((0, 0), (0, SUBLANES - 6), (0, 0)))
    st_mix = jnp.pad(conv_mix0, ((0, 0), (SUBLANES - (CONV_W - 1), 0), (0, 0)))
    xc, small, cst_mix = _inproj_call(
        x, mod8, p["w_norm_mix"], p["w_conv"], (p["w_small"], p["cw_mix"], p["cb_mix"], st_mix),
        nseq=tl["nseq"], tl=tl["tl_in"])
    raw, = _inproj_call(x, mod8, p["w_norm_mix"], p["w_raw"], None, nseq=tl["nseq"], tl=tl["tl_in"])
    h0g = s_ssd0.reshape(b, SSD_GROUPS, SSD_HPG, SSD_STATE, SSD_HEADDIM)
    h0g = jnp.swapaxes(h0g, 2, 3).reshape(b, SSD_GROUPS, SSD_STATE, SSD_GSZ)
    og, s_gdn, ys, hg = _scan_call(xc, raw, small, s_gdn0, h0g, p["gdn_aneg"], p["gdn_dtb"], p["gdn_norm"],
                                   p["ssd_aneg"], p["ssd_dtb"], p["ssd_dexp"], p["ssd_norm"], tc=tl["tc"])
    s_ssd = jnp.swapaxes(hg.reshape(b, SSD_GROUPS, SSD_STATE, SSD_HPG, SSD_HEADDIM), 2, 3)
    s_ssd = s_ssd.reshape(b, SSD_HEADS, SSD_STATE, SSD_HEADDIM)
    st_ffn = jnp.pad(conv_ffn0, ((0, 0), (SUBLANES - (FFN_CONV_W - 1), 0), (0, 0)))
    y, cst_ffn = _mlp_call(
        x, og, ys, raw, mod8, p["wbg"], p["wbs"], p["wo"], p["w_norm_ffn"], p["wup"],
        p["cw_ffn"], p["cb_ffn"], p["wdn"], p["w_norm_final"], st_ffn,
        nseq=tl["nseq"], tl=tl["tl_mlp"], cf=tl["cf"])
    return (y, cst_mix[:, -1, SUBLANES - (CONV_W - 1):], s_gdn, s_ssd,
            cst_ffn[:, SUBLANES - (FFN_CONV_W - 1):])


def kernel(x_prompt, x_sample, state_conv_mix, state_gdn, state_ssd, state_conv_ffn, c_prompt, c_sample,
           w_ada, b_ada, w_norm_mix, w_in, w_conv_mix, b_conv_mix, gdn_a_log, gdn_dt_bias, gdn_norm,
           ssd_a_log, ssd_dt_bias, ssd_d, ssd_norm, w_branch_gdn, w_branch_ssd, w_out,
           w_norm_ffn, w_ffn_up, w_ffn_conv, b_ffn_conv, w_ffn_down, w_norm_final):
    depth = w_in.shape[0]
    assert depth == 1, "single-layer trunk"
    nb = x_prompt.shape[0]
    ndec = x_sample.shape[0]
    l = 0
    p = _prep_params(w_norm_mix[l], w_in[l], w_conv_mix[l], b_conv_mix[l], gdn_a_log[l], gdn_dt_bias[l],
                     gdn_norm[l], ssd_a_log[l], ssd_dt_bias[l], ssd_d[l], ssd_norm[l], w_branch_gdn[l],
                     w_branch_ssd[l], w_out[l], w_norm_ffn[l], w_ffn_up[l], w_ffn_conv[l], b_ffn_conv[l],
                     w_ffn_down[l], w_norm_final)
    c_all = jnp.concatenate([c_prompt, c_sample], axis=0)
    rows = -(-c_all.shape[0] // SUBLANES) * SUBLANES
    mod = _mod_call(jnp.pad(c_all, ((0, rows - c_all.shape[0]), (0, 0))), w_ada[l], b_ada[l])
    dtype = x_prompt.dtype
    zeros = lambda *shape: jnp.zeros(shape, dtype)
    yp, a1, a2, a3, a4 = _trunk(
        x_prompt, mod[:nb],
        zeros(nb, CONV_W - 1, CONV_CH), zeros(nb, GDN_HEADS, GDN_DK, GDN_DV),
        zeros(nb, SSD_HEADS, SSD_STATE, SSD_HEADDIM), zeros(nb, FFN_CONV_W - 1, 2 * D_FF), p)
    ys, b1, b2, b3, b4 = _trunk(
        x_sample, mod[nb:nb + ndec],
        state_conv_mix[l], state_gdn[l], state_ssd[l], state_conv_ffn[l], p)
    return (yp, ys, a1[None], a2[None], a3[None], a4[None], b1[None], b2[None], b3[None], b4[None])
```

```python
import functools

import jax
import jax.numpy as jnp
from jax import lax
from jax.experimental import pallas as pl
from jax.experimental.pallas import tpu as pltpu

F32 = jnp.float32
BF16 = jnp.bfloat16
HIGHEST = lax.Precision.HIGHEST

D_MODEL = 1024
CHUNK = 64
EPS = 1e-6
CONV_W = 4
FFN_CONV_W = 3
GDN_HEADS = 8
GDN_DK = 128
GDN_DV = 128
GDN_QK = GDN_HEADS * GDN_DK
GDN_V = GDN_HEADS * GDN_DV
SSD_INNER = 2 * D_MODEL
SSD_HEADDIM = 64
SSD_HEADS = SSD_INNER // SSD_HEADDIM
SSD_STATE = 128
SSD_GROUPS = 8
SSD_HPG = SSD_HEADS // SSD_GROUPS
SSD_GSZ = SSD_INNER // SSD_GROUPS
SSD_BC = SSD_GROUPS * SSD_STATE
D_FF = 2816
CONV_CH = 2 * GDN_QK + GDN_V + SSD_INNER + 2 * SSD_BC
RAW_CH = GDN_V + SSD_INNER + 2 * D_MODEL
SMALL_CH = 2 * GDN_HEADS + SSD_HEADS
LANES = 128
SUBLANES = 8
LANE_GDN_B = 0
LANE_GDN_A = GDN_HEADS
LANE_SSD_DT = 2 * GDN_HEADS
VMEM_LIMIT_BYTES = 56 * 1024 * 1024


def _sigmoid(x):
    return 1.0 / (1.0 + jnp.exp(-x))


def _silu(x):
    return x * _sigmoid(x)


def _softplus(x):
    return jnp.maximum(x, 0.0) + jnp.log(1.0 + jnp.exp(-jnp.abs(x)))


def _dot(a, b, precision=None):
    return jnp.dot(a, b, preferred_element_type=F32, precision=precision)


def _dot_nt(a, b):
    return lax.dot_general(a, b, (((1,), (1,)), ((), ())), preferred_element_type=F32)


def _dot_tn(a, b):
    return lax.dot_general(a, b, (((0,), (0,)), ((), ())), preferred_element_type=F32)


def _split_bf16(x):
    hi = x.astype(BF16)
    lo = (x - hi.astype(F32)).astype(BF16)
    return hi, lo


def _const_spec(shape):
    zeros = (0,) * len(shape)
    return pl.BlockSpec(shape, lambda *_: zeros)


def _mod_kernel(c_ref, w_ref, b_ref, o_ref):
    o_ref[...] = _dot(_silu(c_ref[...]), w_ref[...], HIGHEST) + b_ref[...]


def _mod_call(c_pad, w_ada, b_ada):
    rows = c_pad.shape[0]
    n = w_ada.shape[1]
    tn = 512
    return pl.pallas_call(
        _mod_kernel,
        grid=(n // tn,),
        in_specs=[pl.BlockSpec((rows, D_MODEL), lambda j: (0, 0)),
                  pl.BlockSpec((D_MODEL, tn), lambda j: (0, j)),
                  pl.BlockSpec((1, tn), lambda j: (0, j))],
        out_specs=pl.BlockSpec((rows, tn), lambda j: (0, j)),
        out_shape=jax.ShapeDtypeStruct((rows, n), F32),
        name="adaln_mod",
    )(c_pad, w_ada, b_ada.reshape(1, n))


COL_TILE = 1024
COL_SUB = 512
ROW_SUB_CONV = 128
ROW_SUB_RAW = 256


def _inproj_kernel(*refs, nseq, tl, conv, ncs):
    if conv:
        (x_ref, mod_ref, wn_ref, w_ref, ws_ref, cw_ref, cb_ref, st_ref,
         out_ref, sm_ref, cst_ref, h_scr, tail_scr, stage_scr) = refs
    else:
        x_ref, mod_ref, wn_ref, w_ref, out_ref, h_scr = refs
    m = pl.program_id(1)
    n = pl.program_id(2)

    @pl.when(n == 0)
    def _():
        for s in range(nseq):
            x = x_ref[s]
            ms = jnp.mean(x * x, axis=-1, keepdims=True)
            md = mod_ref[s]
            h = (x * lax.rsqrt(ms + EPS) * wn_ref[...]) * (1.0 + md[1:2]) + md[0:1]
            hb = h.astype(BF16)
            h_scr[s * tl:(s + 1) * tl, :] = hb
            if conv:
                sm_ref[s] = _dot(hb, ws_ref[...])

    if conv:
        for s in range(nseq):
            @pl.when(m == 0)
            def _():
                tail_scr[n, s] = st_ref[s]

    sr = min(tl, ROW_SUB_CONV if conv else ROW_SUB_RAW)
    for cj in range(ncs):
        cs = slice(cj * COL_SUB, (cj + 1) * COL_SUB)
        for s in range(nseq):
            for lo in range(0, tl, sr):
                acc = _dot(h_scr[s * tl + lo:s * tl + lo + sr, :], w_ref[cj])
                if not conv:
                    out_ref[s, lo:lo + sr, cs] = acc.astype(out_ref.dtype)
                    continue
                stg = stage_scr.at[s, cj % 2]
                if lo == 0:
                    stg[0:SUBLANES] = tail_scr[n, s, :, cs]
                stg[SUBLANES + lo:SUBLANES + lo + sr] = acc
                ext = stg[lo:lo + sr + SUBLANES]
                y = cb_ref[:, cs] + cw_ref[3:4, cs] * ext[SUBLANES:]
                y = y + cw_ref[2:3, cs] * pltpu.roll(ext, 1, 0)[SUBLANES:]
                y = y + cw_ref[1:2, cs] * pltpu.roll(ext, 2, 0)[SUBLANES:]
                y = y + cw_ref[0:1, cs] * pltpu.roll(ext, 3, 0)[SUBLANES:]
                out_ref[s, lo:lo + sr, cs] = _silu(y).astype(out_ref.dtype)
                if lo + sr == tl:
                    new_tail = stg[tl:tl + SUBLANES]
                    tail_scr[n, s, :, cs] = new_tail
                    cst_ref[s, 0, :, cs] = new_tail


def _inproj_call(x, mod8, w_norm, w, conv_args, *, nseq, tl, tile0, ntiles):
    b, t, _ = x.shape
    tn = w.shape[1] * w.shape[3]
    cols = ntiles * tn
    conv = conv_args is not None
    grid = (b // nseq, t // tl, cols // tn)
    kern = functools.partial(_inproj_kernel, nseq=nseq, tl=tl, conv=conv, ncs=tn // COL_SUB)
    in_specs = [
        pl.BlockSpec((nseq, tl, D_MODEL), lambda i, m, n: (i, m, 0)),
        pl.BlockSpec((nseq, SUBLANES, D_MODEL), lambda i, m, n: (i, 0, 0)),
        pl.BlockSpec((1, D_MODEL), lambda i, m, n: (0, 0)),
        pl.BlockSpec((None, tn // COL_SUB, D_MODEL, COL_SUB), lambda i, m, n: (n + tile0, 0, 0, 0)),
    ]
    out_specs = [pl.BlockSpec((None, nseq, tl, tn), lambda i, m, n: (n, i, m, 0))]
    out_shape = [jax.ShapeDtypeStruct((cols // tn, b, t, tn), BF16)]
    scratch = [pltpu.VMEM((nseq * tl, D_MODEL), BF16)]
    args = [x, mod8, w_norm, w]
    if conv:
        in_specs += [
            pl.BlockSpec((D_MODEL, LANES), lambda i, m, n: (0, 0)),
            pl.BlockSpec((SUBLANES, tn), lambda i, m, n: (0, n)),
            pl.BlockSpec((1, tn), lambda i, m, n: (0, n)),
            pl.BlockSpec((nseq, SUBLANES, tn), lambda i, m, n: (i, 0, n)),
        ]
        out_specs += [
            pl.BlockSpec((nseq, tl, LANES), lambda i, m, n: (i, m, 0)),
            pl.BlockSpec((nseq, 1, SUBLANES, tn), lambda i, m, n: (i, m, 0, n)),
        ]
        out_shape += [
            jax.ShapeDtypeStruct((b, t, LANES), F32),
            jax.ShapeDtypeStruct((b, t // tl, SUBLANES, cols), F32),
        ]
        scratch += [
            pltpu.VMEM((cols // tn, nseq, SUBLANES, tn), F32),
            pltpu.VMEM((nseq, 2, tl + SUBLANES, COL_SUB), F32),
        ]
        args += list(conv_args)
    return pl.pallas_call(
        kern,
        grid=grid,
        in_specs=in_specs,
        out_specs=out_specs,
        out_shape=out_shape,
        scratch_shapes=scratch,
        compiler_params=pltpu.CompilerParams(
            dimension_semantics=("arbitrary", "arbitrary", "arbitrary"),
            vmem_limit_bytes=VMEM_LIMIT_BYTES),
        name="inproj_conv" if conv else "inproj_raw",
    )(*args)


INV_BLOCK = 16


def _gdn_stages(q_ref, k_ref, v_ref, z_ref, sm_ref, aneg_ref, dtb_ref, gn_ref, o_ref, s_scr, gsz):
    row_c = lax.broadcasted_iota(jnp.int32, (CHUNK, CHUNK), 0)
    col_c = lax.broadcasted_iota(jnp.int32, (CHUNK, CHUNK), 1)
    tril_b = (row_c >= col_c).astype(BF16)
    qscale = GDN_DK ** -0.5

    heads = range(GDN_HEADS)
    pairs = range(GDN_HEADS // 2)
    hcols = [slice(h * GDN_DK, (h + 1) * GDN_DK) for h in heads]
    row2 = lax.broadcasted_iota(jnp.int32, (CHUNK, 2 * CHUNK), 0)
    col2 = lax.broadcasted_iota(jnp.int32, (CHUNK, 2 * CHUNK), 1) % CHUNK
    first = lax.broadcasted_iota(jnp.int32, (CHUNK, 2 * CHUNK), 1) < CHUNK
    causal2 = row2 >= col2
    strict2 = row2 > col2
    eye2 = (row2 == col2).astype(F32)
    blockdiag2 = (row2 // INV_BLOCK) == (col2 // INV_BLOCK)
    rr = lax.broadcasted_iota(jnp.int32, (2 * CHUNK, 2 * CHUNK), 0)
    cc = lax.broadcasted_iota(jnp.int32, (2 * CHUNK, 2 * CHUNK), 1)
    pair_diag = (rr // CHUNK) == (cc // CHUNK)

    def pair_lhs(a):
        return jnp.concatenate([a[0], a[0], a[1]], axis=1)

    def pair_rhs(b):
        bd = lambda v: jnp.where(pair_diag, jnp.concatenate([v, v], axis=0), 0.0)
        return jnp.concatenate([bd(b[0]), bd(b[1]), bd(b[0])], axis=0)


    def stages(c):
        cis = range(gsz)
        rows = [pl.ds(pl.multiple_of((c * gsz + ci) * CHUNK, CHUNK), CHUNK) for ci in cis]
        beta_all, gam_all, gam_t2, egam_all, elast_all, erem_all = [], [], [], [], [], []
        for ci in cis:
            sm = sm_ref[0, rows[ci], :]
            beta_all.append(_sigmoid(sm))
            g_hi, g_lo = _split_bf16(aneg_ref[...] * _softplus(sm + dtb_ref[...]))
            gam = _dot(tril_b, g_hi) + _dot(tril_b, g_lo)
            gam_all.append(gam)
            gam_t2.append(jnp.concatenate([gam, pltpu.roll(gam, LANES - 1, 1)], axis=0).T)
            egam_all.append(jnp.exp(gam))
            last = gam[CHUNK - 1:CHUNK, :]
            elast_all.append(jnp.exp(last))
            erem_all.append(jnp.exp(last - gam))
        yield
        units = [(ci, h) for ci in cis for h in heads]
        punits = [(ci, p) for ci in cis for p in pairs]
        hcol = lambda arr, lane: arr[:, lane:lane + 1]
        egam_i = {(ci, h): hcol(egam_all[ci], LANE_GDN_A + h) for ci, h in units}
        b_i = {(ci, h): hcol(beta_all[ci], LANE_GDN_B + h) for ci, h in units}
        b_i2, dec2 = {}, {}
        for ci, p in punits:
            lb, lg = LANE_GDN_B + 2 * p, LANE_GDN_A + 2 * p
            b_i2[ci, p] = jnp.where(first, hcol(beta_all[ci], lb), hcol(beta_all[ci], lb + 1))
            gam_i2 = jnp.where(first, hcol(gam_all[ci], lg), hcol(gam_all[ci], lg + 1))
            dec2[ci, p] = jnp.exp(jnp.where(causal2, gam_i2 - gam_t2[ci][lg:lg + 1, :], -jnp.inf))
        yield
        kh, kq = {}, {}
        for ci, h in units:
            qr = q_ref[0, rows[ci], hcols[h]].astype(F32)
            kr = k_ref[0, rows[ci], hcols[h]].astype(F32)
            qn = qr * (lax.rsqrt(jnp.sum(qr * qr, axis=-1, keepdims=True) + EPS) * qscale)
            kn = kr * lax.rsqrt(jnp.sum(kr * kr, axis=-1, keepdims=True) + EPS)
            kh[ci, h] = kn
            kq[ci, h] = jnp.concatenate([kn.astype(BF16), qn.astype(BF16)], axis=0)
        yield
        zk = jnp.zeros((CHUNK, GDN_DK), BF16)
        kqk2 = {}
        for ci, p in punits:
            ka, kb = kq[ci, 2 * p][0:CHUNK], kq[ci, 2 * p + 1][0:CHUNK]
            kbd = jnp.concatenate([jnp.concatenate([ka, zk], axis=1), jnp.concatenate([zk, kb], axis=1)], axis=0)
            kqk2[ci, p] = _dot_nt(jnp.concatenate([kq[ci, 2 * p], kq[ci, 2 * p + 1]], axis=1), kbd)
        yield
        x = {u: jnp.where(strict2, kqk2[u][0:CHUNK] * dec2[u] * (-b_i2[u]), 0.0) for u in punits}
        xd = {u: jnp.where(blockdiag2, x[u], 0.0) for u in punits}
        xn = {u: pair_rhs(_split_bf16(jnp.where(blockdiag2, 0.0, x[u]))) for u in punits}
        yield
        z = {u: _split_bf16(xd[u]) for u in punits}
        pm = {u: eye2 + xd[u] for u in punits}
        for _ in range(3):
            yield
            zf = {u: _dot(pair_lhs(z[u]), pair_rhs(z[u])) for u in punits}
            z = {u: _split_bf16(zf[u]) for u in punits}
            yield
            zr = {u: pair_rhs(z[u]) for u in punits}
            pm = {u: pm[u] + _dot(pair_lhs(_split_bf16(pm[u])), zr[u]) for u in punits}
        yield
        t0 = {u: _split_bf16(pm[u]) for u in punits}
        m = {u: _dot(pair_lhs(t0[u]), xn[u]) for u in punits}
        yield
        ms = {u: _split_bf16(m[u]) for u in punits}
        m2 = {u: pair_rhs(_split_bf16(_dot(pair_lhs(ms[u]), pair_rhs(ms[u])))) for u in punits}
        yield
        ipm = {u: eye2 + m[u] for u in punits}
        qm = {u: ipm[u] + _dot(pair_lhs(_split_bf16(ipm[u])), m2[u]) for u in punits}
        yield
        tinv = {u: pair_lhs(_split_bf16(_dot(pair_lhs(_split_bf16(qm[u])), pair_rhs(t0[u])))) for u in punits}
        yield
        rhs = {(ci, h): _split_bf16(jnp.concatenate(
            [v_ref[0, rows[ci], hcols[h]].astype(F32) * b_i[ci, h], kh[ci, h] * (b_i[ci, h] * egam_i[ci, h])],
            axis=1)) for ci, h in units}
        yield
        zr2 = jnp.zeros((CHUNK, 2 * GDN_DV), BF16)
        sol, qk2 = {}, {}
        for ci, p in punits:
            ra, rb = rhs[ci, 2 * p], rhs[ci, 2 * p + 1]
            bd = lambda i: jnp.concatenate([jnp.concatenate([ra[i], zr2], axis=1),
                                            jnp.concatenate([zr2, rb[i]], axis=1)], axis=0)
            sol2 = _dot(tinv[ci, p], jnp.concatenate([bd(0), bd(1), bd(0)], axis=0))
            sol[ci, 2 * p] = sol2[:, 0:2 * GDN_DV]
            sol[ci, 2 * p + 1] = sol2[:, 2 * GDN_DV:]
            qk2[ci, p] = (kqk2[ci, p][CHUNK:] * dec2[ci, p]).astype(BF16)
        yield
        kdec = {(ci, h): (kh[ci, h] * hcol(erem_all[ci], LANE_GDN_A + h)).astype(BF16) for ci, h in units}
        zu = jnp.zeros((CHUNK, GDN_DV), BF16)
        s = [s_scr[h] for h in heads]
        o = {}
        for ci in cis:
            yield
            sb = [s[h].astype(BF16) for h in heads]
            wq = [jnp.concatenate([sol[ci, h][:, GDN_DV:].astype(BF16), kq[ci, h][CHUNK:]], axis=0) for h in heads]
            ws = [_dot(wq[h], sb[h]) for h in heads]
            yield
            ub = [(sol[ci, h][:, :GDN_DV] - ws[h][0:CHUNK]).astype(BF16) for h in heads]
            for p in pairs:
                ubd = jnp.concatenate([jnp.concatenate([ub[2 * p], zu], axis=1),
                                       jnp.concatenate([zu, ub[2 * p + 1]], axis=1)], axis=0)
                o2 = _dot(qk2[ci, p], ubd)
                o[ci, 2 * p] = egam_i[ci, 2 * p] * ws[2 * p][CHUNK:] + o2[:, :GDN_DV]
                o[ci, 2 * p + 1] = egam_i[ci, 2 * p + 1] * ws[2 * p + 1][CHUNK:] + o2[:, GDN_DV:]
            yield
            s = [hcol(elast_all[ci], LANE_GDN_A + h) * s[h] + _dot_tn(kdec[ci, h], ub[h]) for h in heads]
        yield
        for h in heads:
            s_scr[h] = s[h]
        for ci, h in units:
            zh = z_ref[0, rows[ci], hcols[h]].astype(F32)
            on = (o[ci, h] * lax.rsqrt(jnp.mean(o[ci, h] * o[ci, h], axis=-1, keepdims=True) + EPS)
                  * gn_ref[...] * _silu(zh))
            o_ref[0, rows[ci], hcols[h]] = on.astype(o_ref.dtype)

    return stages


def _group_size(nchunk):
    return 4 if nchunk % 4 == 0 else (2 if nchunk % 2 == 0 else 1)


def _run(*gens):
    live = list(gens)
    while live:
        for g in list(live):
            try:
                next(g)
            except StopIteration:
                live.remove(g)


def _ssd_stages(xlo_ref, xhi_ref, bm_ref, cm_ref, zlo_ref, zhi_ref, sm_ref, aneg_ref, dtb_ref, dexp_ref, nrm_ref,
                y_ref, h_scr):
    pair_w = 2 * SSD_HEADDIM
    row = lax.broadcasted_iota(jnp.int32, (CHUNK, pair_w), 0)
    lane = lax.broadcasted_iota(jnp.int32, (CHUNK, pair_w), 1)
    causal2 = row >= (lane % CHUNK)
    first = lane < SSD_HEADDIM
    rr = lax.broadcasted_iota(jnp.int32, (pair_w, pair_w), 0)
    cc = lax.broadcasted_iota(jnp.int32, (pair_w, pair_w), 1)
    pair_diag = (rr // SSD_HEADDIM) == (cc // SSD_HEADDIM)
    row_c = lax.broadcasted_iota(jnp.int32, (CHUNK, CHUNK), 0)
    col_c = lax.broadcasted_iota(jnp.int32, (CHUNK, CHUNK), 1)
    tril_b = (row_c >= col_c).astype(BF16)
    half = SSD_GROUPS // 2
    groups = range(SSD_GROUPS)
    pairs = range(SSD_HEADS // 2)
    ppg = SSD_HPG // 2

    def stages(c):
        r0 = pl.multiple_of(c * CHUNK, CHUNK)
        rows = pl.ds(r0, CHUNK)
        sm = sm_ref[0, rows, :]
        dt_all = _softplus(sm + dtb_ref[...])
        da_hi, da_lo = _split_bf16(dt_all * aneg_ref[...])
        gam_all = _dot(tril_b, da_hi) + _dot(tril_b, da_lo)
        gam_t2 = jnp.concatenate([gam_all, pltpu.roll(gam_all, LANES - 1, 1)], axis=0).T
        dt_t2 = jnp.concatenate([dt_all, pltpu.roll(dt_all, LANES - 1, 1)], axis=0).T
        yield
        gam_i2, dt_i2 = [], []
        for p in pairs:
            l = LANE_SSD_DT + 2 * p
            gam_i2.append(jnp.where(first, gam_all[:, l:l + 1], gam_all[:, l + 1:l + 2]))
            dt_i2.append(jnp.where(first, dt_all[:, l:l + 1], dt_all[:, l + 1:l + 2]))
        yield
        dec2 = [jnp.exp(jnp.where(causal2, gam_i2[p] - gam_t2[LANE_SSD_DT + 2 * p:LANE_SSD_DT + 2 * p + 1, :],
                                  -jnp.inf)) for p in pairs]
        egam2 = [jnp.exp(gam_i2[p]) for p in pairs]
        last2 = [gam_i2[p][CHUNK - 1:CHUNK, :] for p in pairs]
        wj2 = [jnp.exp(last2[p] - gam_i2[p]) * dt_i2[p] for p in pairs]
        elast2 = [jnp.exp(last2[p]) for p in pairs]

        yield
        xb, xg, bmb, cmb = [], [], [], []
        for g in groups:
            x_ref = xlo_ref if g < half else xhi_ref
            xb.append(x_ref[0, rows, (g % half) * SSD_GSZ:(g % half + 1) * SSD_GSZ])
            xg.append(xb[g].astype(F32))
            bmb.append(bm_ref[0, rows, g * SSD_STATE:(g + 1) * SSD_STATE])
            cmb.append(cm_ref[0, rows, g * SSD_STATE:(g + 1) * SSD_STATE])
        cb2 = [_dot_nt(cmb[g], jnp.concatenate([bmb[g], bmb[g]], axis=0)) for g in groups]
        yield
        ypair = []
        for p in pairs:
            g = p // ppg
            xp = xb[g][:, (p % ppg) * pair_w:(p % ppg + 1) * pair_w]
            xbd = jnp.where(pair_diag, jnp.concatenate([xp, xp], axis=0), 0.0)
            w2 = cb2[g] * dec2[p] * dt_t2[LANE_SSD_DT + 2 * p:LANE_SSD_DT + 2 * p + 1, :]
            ypair.append(_dot(w2.astype(BF16), xbd))
        yield
        hg = [h_scr[g] for g in groups]
        ch = [_dot(cmb[g], hg[g].astype(BF16)) for g in groups]
        yield
        for g in groups:
            ps = range(g * ppg, (g + 1) * ppg)
            wj = jnp.concatenate([wj2[p] for p in ps], axis=1)
            elast = jnp.concatenate([elast2[p] for p in ps], axis=1)
            h_scr[g] = elast * hg[g] + _dot_tn(bmb[g], (xg[g] * wj).astype(BF16))
        yield
        for g in groups:
            ps = range(g * ppg, (g + 1) * ppg)
            z_ref = zlo_ref if g < half else zhi_ref
            ocols = slice(g * SSD_GSZ, (g + 1) * SSD_GSZ)
            egam = jnp.concatenate([egam2[p] for p in ps], axis=1)
            y = jnp.concatenate([ypair[p] for p in ps], axis=1) + ch[g] * egam
            y = y + dexp_ref[:, ocols] * xg[g]
            yz = y * _silu(z_ref[0, rows, (g % half) * SSD_GSZ:(g % half + 1) * SSD_GSZ].astype(F32))
            yn = yz * lax.rsqrt(jnp.mean(yz * yz, axis=-1, keepdims=True) + EPS) * nrm_ref[:, ocols]
            y_ref[0, rows, ocols] = yn.astype(y_ref.dtype)

    return stages


def _scan_kernel(q_ref, k_ref, v_ref, zg_ref, xlo_ref, xhi_ref, bm_ref, cm_ref, zlo_ref, zhi_ref, sm_ref,
                 s0_ref, h0_ref, ganeg_ref, gdtb_ref, gn_ref, saneg_ref, sdtb_ref, dexp_ref, nrm_ref,
                 o_ref, sout_ref, y_ref, hout_ref, s_scr, h_scr, *, nchunk):
    t = pl.program_id(1)

    @pl.when(t == 0)
    def _():
        s_scr[...] = s0_ref[0]
        h_scr[...] = h0_ref[0]

    gsz = _group_size(nchunk)
    gdn = _gdn_stages(q_ref, k_ref, v_ref, zg_ref, sm_ref, ganeg_ref, gdtb_ref, gn_ref, o_ref, s_scr, gsz)
    ssd = _ssd_stages(xlo_ref, xhi_ref, bm_ref, cm_ref, zlo_ref, zhi_ref, sm_ref, saneg_ref, sdtb_ref, dexp_ref,
                      nrm_ref, y_ref, h_scr)

    def ssd_chunks(c):
        for ci in range(gsz):
            yield from ssd(c * gsz + ci)

    def chunk_group(c, carry):
        _run(gdn(c), ssd_chunks(c))
        return carry

    lax.fori_loop(0, nchunk // gsz, chunk_group, 0)

    @pl.when(t == pl.num_programs(1) - 1)
    def _():
        sout_ref[0] = s_scr[...]
        hout_ref[0] = h_scr[...]


def _scan_call(xc, raw, small, s0, h0g, ganeg, gdtb, gnorm, saneg, sdtb, dexp, nrm, *, tc):
    _, b, t, _ = xc.shape
    kern = functools.partial(_scan_kernel, nchunk=tc // CHUNK)
    xct = lambda j: pl.BlockSpec((None, 1, tc, COL_TILE), lambda i, m: (j, i, m, 0))
    gstate = pl.BlockSpec((1, GDN_HEADS, GDN_DK, GDN_DV), lambda i, m: (i, 0, 0, 0))
    sstate = pl.BlockSpec((1, SSD_GROUPS, SSD_STATE, SSD_GSZ), lambda i, m: (i, 0, 0, 0))
    return pl.pallas_call(
        kern,
        grid=(b, t // tc),
        in_specs=[xct(0), xct(1), xct(2), xct(0), xct(3), xct(4), xct(5), xct(6), xct(1), xct(2),
                  pl.BlockSpec((1, tc, LANES), lambda i, m: (i, m, 0)),
                  gstate, sstate,
                  _const_spec((1, LANES)), _const_spec((1, LANES)), _const_spec((1, GDN_DV)),
                  _const_spec((1, LANES)), _const_spec((1, LANES)),
                  _const_spec((1, SSD_INNER)), _const_spec((1, SSD_INNER))],
        out_specs=[pl.BlockSpec((1, tc, GDN_V), lambda i, m: (i, m, 0)), gstate,
                   pl.BlockSpec((1, tc, SSD_INNER), lambda i, m: (i, m, 0)), sstate],
        out_shape=[jax.ShapeDtypeStruct((b, t, GDN_V), BF16),
                   jax.ShapeDtypeStruct((b, GDN_HEADS, GDN_DK, GDN_DV), F32),
                   jax.ShapeDtypeStruct((b, t, SSD_INNER), BF16),
                   jax.ShapeDtypeStruct((b, SSD_GROUPS, SSD_STATE, SSD_GSZ), F32)],
        scratch_shapes=[pltpu.VMEM((GDN_HEADS, GDN_DK, GDN_DV), F32),
                        pltpu.VMEM((SSD_GROUPS, SSD_STATE, SSD_GSZ), F32)],
        compiler_params=pltpu.CompilerParams(
            dimension_semantics=("arbitrary", "arbitrary"),
            vmem_limit_bytes=VMEM_LIMIT_BYTES),
        name="gdn_ssd_scan",
    )(xc, xc, xc, raw, xc, xc, xc, xc, raw, raw, small, s0, h0g, ganeg, gdtb, gnorm, saneg, sdtb, dexp, nrm)


def _mlp_kernel(x_ref, og_ref, ys_ref, ga_ref, gb_ref, mod_ref,
                wbg_ref, wbs_ref, wo_ref, wn2_ref, wup_ref, cw_ref, cb_ref, wdn_ref, wnf_ref, st_ref,
                out_ref, cst_ref, tail_scr, sa_scr, sv_scr, *, nseq, tl, cf):
    m = pl.program_id(1)
    for s in range(nseq):
        @pl.when(m == 0)
        def _():
            tail_scr[s] = st_ref[s]

        md = mod_ref[s]
        mixed = (_sigmoid(ga_ref[s].astype(F32)) * _dot(og_ref[s], wbg_ref[...])
                 + _sigmoid(gb_ref[s].astype(F32)) * _dot(ys_ref[s], wbs_ref[...]))
        x1 = x_ref[s] + md[2:3] * _dot(mixed.astype(BF16), wo_ref[...])
        ms = jnp.mean(x1 * x1, axis=-1, keepdims=True)
        h2 = ((x1 * lax.rsqrt(ms + EPS) * wn2_ref[...]) * (1.0 + md[4:5]) + md[3:4]).astype(BF16)

        def conv(stage, u, cols):
            stage[0:SUBLANES] = tail_scr[s, :, cols]
            stage[SUBLANES:SUBLANES + tl] = u
            y = cb_ref[:, cols] + cw_ref[2:3, cols] * stage[8:8 + tl]
            y = y + cw_ref[1:2, cols] * stage[7:7 + tl]
            y = y + cw_ref[0:1, cols] * stage[6:6 + tl]
            tail_scr[s, :, cols] = stage[tl:tl + SUBLANES]
            return y

        def up(j):
            ca = slice(j * cf, (j + 1) * cf)
            cv = slice(D_FF + j * cf, D_FF + (j + 1) * cf)
            return ca, cv, _dot(h2, wup_ref[:, ca]), _dot(h2, wup_ref[:, cv])

        acc = jnp.zeros((tl, D_MODEL), F32)
        nxt = up(0)
        for j in range(D_FF // cf):
            ca, cv, pa, pv = nxt
            if j + 1 < D_FF // cf:
                nxt = up(j + 1)
            ua = conv(sa_scr.at[j % 2], pa, ca)
            uv = conv(sv_scr.at[j % 2], pv, cv)
            act = (_silu(ua) * uv).astype(BF16)
            acc = acc + _dot(act, wdn_ref[ca, :])
        x2 = x1 + md[5:6] * acc
        ms2 = jnp.mean(x2 * x2, axis=-1, keepdims=True)
        out_ref[s] = x2 * lax.rsqrt(ms2 + EPS) * wnf_ref[...]
        cst_ref[s] = tail_scr[s]


def _mlp_call(x, og, ys, raw, mod8, wbg, wbs, wo, wn2, wup, cw8, cb, wdn, wnf, st8, *, nseq, tl, cf):
    b, t, _ = x.shape
    kern = functools.partial(_mlp_kernel, nseq=nseq, tl=tl, cf=cf)
    tok = lambda width, j: pl.BlockSpec((nseq, tl, width), lambda i, m: (i, m, j))
    raw_tile = lambda j: pl.BlockSpec((None, nseq, tl, COL_TILE), lambda i, m: (j, i, m, 0))
    per_seq = lambda width: pl.BlockSpec((nseq, SUBLANES, width), lambda i, m: (i, 0, 0))

    def resident(shape):
        zeros = (0,) * len(shape)
        return pl.BlockSpec(shape, lambda i, m: zeros, pipeline_mode=pl.Buffered(1))

    return pl.pallas_call(
        kern,
        grid=(b // nseq, t // tl),
        in_specs=[tok(D_MODEL, 0), tok(GDN_V, 0), tok(SSD_INNER, 0), raw_tile(3), raw_tile(4),
                  per_seq(D_MODEL),
                  resident((GDN_V, D_MODEL)), resident((SSD_INNER, D_MODEL)), resident((D_MODEL, D_MODEL)),
                  resident((1, D_MODEL)), resident((D_MODEL, 2 * D_FF)),
                  resident((SUBLANES, 2 * D_FF)), resident((1, 2 * D_FF)),
                  resident((D_FF, D_MODEL)), resident((1, D_MODEL)),
                  per_seq(2 * D_FF)],
        out_specs=[tok(D_MODEL, 0), per_seq(2 * D_FF)],
        out_shape=[jax.ShapeDtypeStruct((b, t, D_MODEL), F32),
                   jax.ShapeDtypeStruct((b, SUBLANES, 2 * D_FF), F32)],
        scratch_shapes=[pltpu.VMEM((nseq, SUBLANES, 2 * D_FF), F32),
                        pltpu.VMEM((2, tl + SUBLANES, cf), F32),
                        pltpu.VMEM((2, tl + SUBLANES, cf), F32)],
        compiler_params=pltpu.CompilerParams(
            dimension_semantics=("arbitrary", "arbitrary"),
            vmem_limit_bytes=VMEM_LIMIT_BYTES),
        name="merge_ffn",
    )(x, og, ys, raw, raw, mod8, wbg, wbs, wo, wn2, wup, cw8, cb, wdn, wnf, st8)


def _lane_vec(values, offset):
    return jnp.zeros((1, LANES), F32).at[0, offset:offset + values.shape[0]].set(values.astype(F32))


RETILE_ROWS = 128
O_SMALL = CONV_CH + GDN_V + SSD_INNER
O_MERGE = O_SMALL + SMALL_CH
SUB_TILE_STARTS = tuple(range(0, O_SMALL, COL_SUB)) + tuple(O_MERGE + c for c in range(0, 2 * D_MODEL, COL_SUB))


def _retile_kernel(w_ref, o_ref, s_ref):
    for j, c0 in enumerate(SUB_TILE_STARTS):
        o_ref[j] = w_ref[:, c0:c0 + COL_SUB].astype(BF16)
    sm = w_ref[:, O_SMALL:O_SMALL + LANES]
    lane = lax.broadcasted_iota(jnp.int32, sm.shape, 1)
    s_ref[...] = jnp.where(lane < SMALL_CH, sm, 0.0).astype(BF16)


def _retile_call(w_in):
    nsub = len(SUB_TILE_STARTS)
    return pl.pallas_call(
        _retile_kernel,
        grid=(D_MODEL // RETILE_ROWS,),
        in_specs=[pl.BlockSpec((RETILE_ROWS, w_in.shape[1]), lambda i: (i, 0))],
        out_specs=[pl.BlockSpec((nsub, RETILE_ROWS, COL_SUB), lambda i: (0, i, 0)),
                   pl.BlockSpec((RETILE_ROWS, LANES), lambda i: (i, 0))],
        out_shape=[jax.ShapeDtypeStruct((nsub, D_MODEL, COL_SUB), BF16),
                   jax.ShapeDtypeStruct((D_MODEL, LANES), BF16)],
        compiler_params=pltpu.CompilerParams(vmem_limit_bytes=VMEM_LIMIT_BYTES),
        name="retile_w_in",
    )(w_in)


def _prep_params(w_norm_mix, w_in, w_conv_mix, b_conv_mix, gdn_a_log, gdn_dt_bias, gdn_norm,
                 ssd_a_log, ssd_dt_bias, ssd_d, ssd_norm, w_branch_gdn, w_branch_ssd, w_out,
                 w_norm_ffn, w_ffn_up, w_ffn_conv, b_ffn_conv, w_ffn_down, w_norm_final):
    w_sub, w_small = _retile_call(w_in)
    ncs = COL_TILE // COL_SUB
    return dict(
        w_norm_mix=w_norm_mix.reshape(1, D_MODEL),
        w_proj=w_sub.reshape(len(SUB_TILE_STARTS) // ncs, ncs, D_MODEL, COL_SUB),
        w_small=w_small,
        cw_mix=jnp.pad(w_conv_mix, ((0, SUBLANES - CONV_W), (0, 0))),
        cb_mix=b_conv_mix.reshape(1, CONV_CH),
        gdn_aneg=_lane_vec(-jnp.exp(gdn_a_log.astype(F32)), LANE_GDN_A),
        gdn_dtb=_lane_vec(gdn_dt_bias, LANE_GDN_A),
        gdn_norm=gdn_norm.reshape(1, GDN_DV).astype(F32),
        ssd_aneg=_lane_vec(-jnp.exp(ssd_a_log.astype(F32)), LANE_SSD_DT),
        ssd_dtb=_lane_vec(ssd_dt_bias, LANE_SSD_DT),
        ssd_dexp=jnp.repeat(ssd_d.astype(F32), SSD_HEADDIM).reshape(1, SSD_INNER),
        ssd_norm=ssd_norm.reshape(1, SSD_INNER).astype(F32),
        wbg=w_branch_gdn.astype(BF16), wbs=w_branch_ssd.astype(BF16), wo=w_out.astype(BF16),
        w_norm_ffn=w_norm_ffn.reshape(1, D_MODEL),
        wup=w_ffn_up.astype(BF16),
        cw_ffn=jnp.pad(w_ffn_conv, ((0, SUBLANES - FFN_CONV_W), (0, 0))),
        cb_ffn=b_ffn_conv.reshape(1, 2 * D_FF),
        wdn=w_ffn_down.astype(BF16),
        w_norm_final=w_norm_final.reshape(1, D_MODEL),
    )


def _tiles(b, t):
    if t <= CHUNK:
        return dict(nseq=b, tl_in=t, tl_mlp=t, tc=t, cf=256)
    return dict(nseq=1, tl_in=min(t, 1024), tl_mlp=min(t, 512), tc=min(t, 512), cf=256)


def _trunk(x, mod, conv_mix0, s_gdn0, s_ssd0, conv_ffn0, p):
    b, t, _ = x.shape
    tl = _tiles(b, t)
    mod8 = jnp.pad(mod.reshape(b, 6, D_MODEL), ((0, 0), (0, SUBLANES - 6), (0, 0)))
    st_mix = jnp.pad(conv_mix0, ((0, 0), (SUBLANES - (CONV_W - 1), 0), (0, 0)))
    xc, small, cst_mix = _inproj_call(
        x, mod8, p["w_norm_mix"], p["w_proj"], (p["w_small"], p["cw_mix"], p["cb_mix"], st_mix),
        nseq=tl["nseq"], tl=tl["tl_in"], tile0=0, ntiles=CONV_CH // COL_TILE)
    raw, = _inproj_call(x, mod8, p["w_norm_mix"], p["w_proj"], None, nseq=tl["nseq"], tl=tl["tl_in"],
                        tile0=CONV_CH // COL_TILE, ntiles=RAW_CH // COL_TILE)
    h0g = s_ssd0.reshape(b, SSD_GROUPS, SSD_HPG, SSD_STATE, SSD_HEADDIM)
    h0g = jnp.swapaxes(h0g, 2, 3).reshape(b, SSD_GROUPS, SSD_STATE, SSD_GSZ)
    og, s_gdn, ys, hg = _scan_call(xc, raw, small, s_gdn0, h0g, p["gdn_aneg"], p["gdn_dtb"], p["gdn_norm"],
                                   p["ssd_aneg"], p["ssd_dtb"], p["ssd_dexp"], p["ssd_norm"], tc=tl["tc"])
    s_ssd = jnp.swapaxes(hg.reshape(b, SSD_GROUPS, SSD_STATE, SSD_HPG, SSD_HEADDIM), 2, 3)
    s_ssd = s_ssd.reshape(b, SSD_HEADS, SSD_STATE, SSD_HEADDIM)
    st_ffn = jnp.pad(conv_ffn0, ((0, 0), (SUBLANES - (FFN_CONV_W - 1), 0), (0, 0)))
    y, cst_ffn = _mlp_call(
        x, og, ys, raw, mod8, p["wbg"], p["wbs"], p["wo"], p["w_norm_ffn"], p["wup"],
        p["cw_ffn"], p["cb_ffn"], p["wdn"], p["w_norm_final"], st_ffn,
        nseq=tl["nseq"], tl=tl["tl_mlp"], cf=tl["cf"])
    return (y, cst_mix[:, -1, SUBLANES - (CONV_W - 1):], s_gdn, s_ssd,
            cst_ffn[:, SUBLANES - (FFN_CONV_W - 1):])


def kernel(x_prompt, x_sample, state_conv_mix, state_gdn, state_ssd, state_conv_ffn, c_prompt, c_sample,
           w_ada, b_ada, w_norm_mix, w_in, w_conv_mix, b_conv_mix, gdn_a_log, gdn_dt_bias, gdn_norm,
           ssd_a_log, ssd_dt_bias, ssd_d, ssd_norm, w_branch_gdn, w_branch_ssd, w_out,
           w_norm_ffn, w_ffn_up, w_ffn_conv, b_ffn_conv, w_ffn_down, w_norm_final):
    depth = w_in.shape[0]
    assert depth == 1, "single-layer trunk"
    nb = x_prompt.shape[0]
    ndec = x_sample.shape[0]
    l = 0
    p = _prep_params(w_norm_mix[l], w_in[l], w_conv_mix[l], b_conv_mix[l], gdn_a_log[l], gdn_dt_bias[l],
                     gdn_norm[l], ssd_a_log[l], ssd_dt_bias[l], ssd_d[l], ssd_norm[l], w_branch_gdn[l],
                     w_branch_ssd[l], w_out[l], w_norm_ffn[l], w_ffn_up[l], w_ffn_conv[l], b_ffn_conv[l],
                     w_ffn_down[l], w_norm_final)
    c_all = jnp.concatenate([c_prompt, c_sample], axis=0)
    rows = -(-c_all.shape[0] // SUBLANES) * SUBLANES
    mod = _mod_call(jnp.pad(c_all, ((0, rows - c_all.shape[0]), (0, 0))), w_ada[l], b_ada[l])
    dtype = x_prompt.dtype
    zeros = lambda *shape: jnp.zeros(shape, dtype)
    yp, a1, a2, a3, a4 = _trunk(
        x_prompt, mod[:nb],
        zeros(nb, CONV_W - 1, CONV_CH), zeros(nb, GDN_HEADS, GDN_DK, GDN_DV),
        zeros(nb, SSD_HEADS, SSD_STATE, SSD_HEADDIM), zeros(nb, FFN_CONV_W - 1, 2 * D_FF), p)
    ys, b1, b2, b3, b4 = _trunk(
        x_sample, mod[nb:nb + ndec],
        state_conv_mix[l], state_gdn[l], state_ssd[l], state_conv_ffn[l], p)
    return (yp, ys, a1[None], a2[None], a3[None], a4[None], b1[None], b2[None], b3[None], b4[None])
```
